```python
import jax, jax.numpy as jnp
from jax import lax
import numpy as np

D_MODEL = 4096
BATCH = 4
SEQ = 2048
DEPTH = 1
DEC_BATCH = 128
DEC_SEQ = 8
PAST_LEN = 16384
PAGE_SIZE = 128

D_LRU = D_MODEL
LRU_BLOCKS = 16
LRU_BW = D_LRU // LRU_BLOCKS
CONV_W = 4
LRU_C = 8.0
RET_DK = 256
RET_HEADS = D_MODEL // RET_DK
RET_DV = 2 * D_MODEL // RET_HEADS
RET_CHUNK = 128
ROPE_BASE = 10000.0
N_MEM = 256
XA_HEADS = 4
XA_HD = D_MODEL // XA_HEADS
N_EXPERTS = 32
TOP_K = 4
D_FF = D_MODEL
SWIGLU_LIMIT = 7.0
SWIGLU_ALPHA = 1.702
EPS = 1e-6
IN_SIZES = (D_LRU, D_LRU, RET_HEADS * RET_DK, RET_HEADS * RET_DK, RET_HEADS * RET_DV, RET_HEADS * RET_DV, D_MODEL, D_MODEL)
N_IN = sum(IN_SIZES)

kernel_name = "hawk_retnet_memxattn_moe_step"


def rmsnorm(x, g):
    xf = x.astype(jnp.float32)
    y = xf * lax.rsqrt(jnp.mean(xf * xf, axis=-1, keepdims=True) + EPS)
    return (y * g.astype(jnp.float32)).astype(x.dtype)


def rotary(x, pos):
    half = x.shape[-1] // 2
    inv = ROPE_BASE ** (-jnp.arange(half, dtype=jnp.float32) / half)
    ang = pos.astype(jnp.float32)[:, None] * inv[None, :]
    cos = jnp.cos(ang)[None, :, None, :]
    sin = jnp.sin(ang)[None, :, None, :]
    xf = x.astype(jnp.float32)
    x1, x2 = xf[..., :half], xf[..., half:]
    return jnp.concatenate([x1 * cos - x2 * sin, x1 * sin + x2 * cos], axis=-1)


def causal_conv(u, buf, w, b):
    T = u.shape[1]
    ext = jnp.concatenate([buf.astype(u.dtype), u], axis=1)
    out = b + sum(ext[:, j:j + T] * w[j] for j in range(CONV_W))
    return out, ext[:, -(CONV_W - 1):]


def rg_lru(u, h0, wa, ba, wx, bx, lam):
    B, T, C = u.shape
    uf = u.astype(jnp.float32)
    ub = uf.reshape(B, T, LRU_BLOCKS, LRU_BW)
    r = jax.nn.sigmoid(jnp.einsum('btnc,ncd->btnd', ub, wa.astype(jnp.float32)).reshape(B, T, C) + ba)
    i = jax.nn.sigmoid(jnp.einsum('btnc,ncd->btnd', ub, wx.astype(jnp.float32)).reshape(B, T, C) + bx)
    log_a = -LRU_C * r * jax.nn.softplus(-lam.astype(jnp.float32))
    a = jnp.exp(log_a)
    b_in = jnp.sqrt(-jnp.expm1(2.0 * log_a)) * (i * uf)

    def step(h, ab):
        a_t, b_t = ab
        h = a_t * h + b_t
        return h, h

    hT, hs = lax.scan(step, h0.astype(jnp.float32), (a.swapaxes(0, 1), b_in.swapaxes(0, 1)))
    return hs.swapaxes(0, 1), hT


def retention(q, k, v, S0):
    B, T = q.shape[:2]
    C = RET_CHUNK if T % RET_CHUNK == 0 else T
    n = T // C
    log_g = jnp.log1p(-jnp.exp2(-5.0 - jnp.arange(RET_HEADS, dtype=jnp.float32)))
    idx = jnp.arange(C, dtype=jnp.float32)
    diff = idx[:, None] - idx[None, :]
    dec = jnp.where(diff[None] >= 0, jnp.exp(jnp.maximum(diff, 0.0)[None] * log_g[:, None, None]), 0.0)
    xi = jnp.exp((idx + 1.0)[:, None] * log_g[None, :])
    zeta = jnp.exp((C - 1.0 - idx)[:, None] * log_g[None, :])
    g_chunk = jnp.exp(C * log_g)

    def step(S, qkv):
        qc, kc, vc = qkv
        scores = jnp.einsum('blhk,bmhk->bhlm', qc, kc) * dec[None]
        o = jnp.einsum('bhlm,bmhv->blhv', scores, vc)
        o = o + jnp.einsum('blhk,bhkv->blhv', qc, S) * xi[None, :, :, None]
        S = g_chunk[None, :, None, None] * S + jnp.einsum('blhk,blhv->bhkv', kc * zeta[None, :, :, None], vc)
        return S, o

    def chunks(a):
        return a.reshape(B, n, C, *a.shape[2:]).swapaxes(0, 1)

    S_T, o = lax.scan(step, S0.astype(jnp.float32), (chunks(q), chunks(k), chunks(v)))
    return o.swapaxes(0, 1).reshape(B, T, RET_HEADS, RET_DV), S_T


def mixer(xn, pos, conv_buf, h0, S0, w_in, conv_w, conv_b, lru_wa, lru_ba, lru_wx, lru_bx, lru_lambda,
          ret_norm_g, w_lru_branch, w_ret_branch, w_mix_out):
    B, T, _ = xn.shape
    split_at = [int(s) for s in np.cumsum(IN_SIZES)[:-1]]
    u, g_lru, q, k, v, g_ret, gate_lru, gate_ret = jnp.split(xn @ w_in, split_at, axis=-1)
    u_conv, new_buf = causal_conv(u, conv_buf, conv_w, conv_b)
    hs, hT = rg_lru(u_conv, h0, lru_wa, lru_ba, lru_wx, lru_bx, lru_lambda)
    y_lru = jax.nn.gelu(g_lru) * hs.astype(xn.dtype)
    q = rotary(q.reshape(B, T, RET_HEADS, RET_DK), pos)
    k = rotary(k.reshape(B, T, RET_HEADS, RET_DK), pos) * (RET_DK ** -0.5)
    v = v.reshape(B, T, RET_HEADS, RET_DV).astype(jnp.float32)
    o, S_T = retention(q, k, v, S0)
    o = o * lax.rsqrt(jnp.mean(o * o, axis=-1, keepdims=True) + EPS) * ret_norm_g.astype(jnp.float32)
    y_ret = jax.nn.silu(g_ret) * o.reshape(B, T, RET_HEADS * RET_DV).astype(xn.dtype)
    merged = jax.nn.sigmoid(gate_lru) * (y_lru @ w_lru_branch) + jax.nn.sigmoid(gate_ret) * (y_ret @ w_ret_branch)
    return merged @ w_mix_out, new_buf, hT, S_T


def mem_kv(mem, norm_mem_g, xa_wk, xa_wv):
    B = mem.shape[0]
    mn = rmsnorm(mem, norm_mem_g)
    mk = (mn @ xa_wk).reshape(B, N_MEM, XA_HEADS, XA_HD)
    mv = (mn @ xa_wv).reshape(B, N_MEM, XA_HEADS, XA_HD)
    return mk, mv


def cross_attn(xn, mk, mv, xa_wq, xa_wo):
    B, T, _ = xn.shape
    q = (xn @ xa_wq).reshape(B, T, XA_HEADS, XA_HD)
    s = jnp.einsum('bthd,bmhd->bhtm', q, mk.astype(xn.dtype)).astype(jnp.float32) * (XA_HD ** -0.5)
    p = jax.nn.softmax(s, axis=-1).astype(xn.dtype)
    o = jnp.einsum('bhtm,bmhd->bthd', p, mv.astype(xn.dtype)).reshape(B, T, D_MODEL)
    return o @ xa_wo


def moe(xn, router_w, router_b, moe_w_gu, moe_b_gu, moe_w_dn, moe_b_dn):
    B, T, D = xn.shape
    xt = xn.reshape(B * T, D)
    logits = (xt @ router_w + router_b).astype(jnp.float32)
    top_v, top_i = lax.top_k(logits, TOP_K)
    probs = jax.nn.softmax(top_v, axis=-1)
    combine = jnp.einsum('nk,nke->ne', probs, jax.nn.one_hot(top_i, N_EXPERTS, dtype=jnp.float32))
    out = jnp.zeros((B * T, D), jnp.float32)
    for e in range(N_EXPERTS):
        h = xt @ moe_w_gu[e] + moe_b_gu[e]
        gate = jnp.minimum(h[:, 0::2], SWIGLU_LIMIT)
        up = jnp.clip(h[:, 1::2], -SWIGLU_LIMIT, SWIGLU_LIMIT)
        act = (up + 1.0) * (gate * jax.nn.sigmoid(SWIGLU_ALPHA * gate))
        out = out + combine[:, e:e + 1] * (act @ moe_w_dn[e] + moe_b_dn[e]).astype(jnp.float32)
    return out.astype(xn.dtype).reshape(B, T, D)


def layer(x, pos, conv_buf, h0, S0, mk, mv, norm_mix_g, w_in, conv_w, conv_b, lru_wa, lru_ba, lru_wx, lru_bx,
          lru_lambda, ret_norm_g, w_lru_branch, w_ret_branch, w_mix_out, norm_xa_g, xa_wq, xa_wo, norm_ffn_g,
          router_w, router_b, moe_w_gu, moe_b_gu, moe_w_dn, moe_b_dn):
    m, new_buf, hT, S_T = mixer(rmsnorm(x, norm_mix_g), pos, conv_buf, h0, S0, w_in, conv_w, conv_b, lru_wa, lru_ba,
                                lru_wx, lru_bx, lru_lambda, ret_norm_g, w_lru_branch, w_ret_branch, w_mix_out)
    h = x + m
    h = h + cross_attn(rmsnorm(h, norm_xa_g), mk, mv, xa_wq, xa_wo)
    h = h + moe(rmsnorm(h, norm_ffn_g), router_w, router_b, moe_w_gu, moe_b_gu, moe_w_dn, moe_b_dn)
    return h, new_buf, hT, S_T


def setup_inputs(seed: int = 0) -> dict:
    key = jax.random.key(seed)
    ks = iter(jax.random.split(key, 64))
    f32 = jnp.float32

    def nrm(shape, scale):
        return jax.random.normal(next(ks), shape, f32) * scale

    def gain(shape):
        return 1.0 + nrm(shape, 0.02)

    L = DEPTH
    u = jax.random.uniform(next(ks), (L, D_LRU), f32, 0.9, 0.999)
    a_base = u ** (1.0 / LRU_C)
    lru_lambda = jnp.log(a_base) - jnp.log1p(-a_base)
    return {
        "x_prompt": nrm((BATCH, SEQ, D_MODEL), 1.0),
        "x_sample": nrm((DEC_BATCH, DEC_SEQ, D_MODEL), 1.0),
        "mem_prompt": nrm((BATCH, N_MEM, D_MODEL), 1.0),
        "state_conv": nrm((L, DEC_BATCH, CONV_W - 1, D_LRU), 1.0),
        "state_lru": nrm((L, DEC_BATCH, D_LRU), 0.5),
        "state_ret": nrm((L, DEC_BATCH, RET_HEADS, RET_DK, RET_DV), 1.0),
        "cache_mem_k": nrm((L, DEC_BATCH, N_MEM, XA_HEADS, XA_HD), 1.0),
        "cache_mem_v": nrm((L, DEC_BATCH, N_MEM, XA_HEADS, XA_HD), 1.0),
        "norm_mix_g": gain((L, D_MODEL)),
        "w_in": nrm((L, D_MODEL, N_IN), D_MODEL ** -0.5),
        "conv_w": nrm((L, CONV_W, D_LRU), CONV_W ** -0.5),
        "conv_b": nrm((L, D_LRU), 0.01),
        "lru_wa": nrm((L, LRU_BLOCKS, LRU_BW, LRU_BW), LRU_BW ** -0.5),
        "lru_ba": nrm((L, D_LRU), 0.01),
        "lru_wx": nrm((L, LRU_BLOCKS, LRU_BW, LRU_BW), LRU_BW ** -0.5),
        "lru_bx": nrm((L, D_LRU), 0.01),
        "lru_lambda": lru_lambda,
        "ret_norm_g": gain((L, RET_HEADS, RET_DV)),
        "w_lru_branch": nrm((L, D_LRU, D_MODEL), D_LRU ** -0.5),
        "w_ret_branch": nrm((L, RET_HEADS * RET_DV, D_MODEL), (RET_HEADS * RET_DV) ** -0.5),
        "w_mix_out": nrm((L, D_MODEL, D_MODEL), D_MODEL ** -0.5),
        "norm_xa_g": gain((L, D_MODEL)),
        "norm_mem_g": gain((L, D_MODEL)),
        "xa_wq": nrm((L, D_MODEL, D_MODEL), D_MODEL ** -0.5),
        "xa_wk": nrm((L, D_MODEL, D_MODEL), D_MODEL ** -0.5),
        "xa_wv": nrm((L, D_MODEL, D_MODEL), D_MODEL ** -0.5),
        "xa_wo": nrm((L, D_MODEL, D_MODEL), D_MODEL ** -0.5),
        "norm_ffn_g": gain((L, D_MODEL)),
        "router_w": nrm((L, D_MODEL, N_EXPERTS), D_MODEL ** -0.5),
        "router_b": nrm((L, N_EXPERTS), 0.01),
        "moe_w_gu": nrm((L, N_EXPERTS, D_MODEL, 2 * D_FF), D_MODEL ** -0.5),
        "moe_b_gu": nrm((L, N_EXPERTS, 2 * D_FF), 0.01),
        "moe_w_dn": nrm((L, N_EXPERTS, D_FF, D_MODEL), D_FF ** -0.5),
        "moe_b_dn": nrm((L, N_EXPERTS, D_MODEL), 0.01),
        "norm_final_g": gain((D_MODEL,)),
    }


def reference(x_prompt, x_sample, mem_prompt, state_conv, state_lru, state_ret, cache_mem_k, cache_mem_v,
              norm_mix_g, w_in, conv_w, conv_b, lru_wa, lru_ba, lru_wx, lru_bx, lru_lambda, ret_norm_g,
              w_lru_branch, w_ret_branch, w_mix_out, norm_xa_g, norm_mem_g, xa_wq, xa_wk, xa_wv, xa_wo,
              norm_ffn_g, router_w, router_b, moe_w_gu, moe_b_gu, moe_w_dn, moe_b_dn, norm_final_g):
    pos_p = jnp.arange(SEQ, dtype=jnp.int32)
    pos_s = PAST_LEN + jnp.arange(DEC_SEQ, dtype=jnp.int32)
    hp, hs_ = x_prompt, x_sample
    p_conv, p_lru, p_ret, p_mk, p_mv, s_conv, s_lru, s_ret = [], [], [], [], [], [], [], []
    for l in range(DEPTH):
        lw = dict(norm_mix_g=norm_mix_g[l], w_in=w_in[l], conv_w=conv_w[l], conv_b=conv_b[l], lru_wa=lru_wa[l],
                  lru_ba=lru_ba[l], lru_wx=lru_wx[l], lru_bx=lru_bx[l], lru_lambda=lru_lambda[l],
                  ret_norm_g=ret_norm_g[l], w_lru_branch=w_lru_branch[l], w_ret_branch=w_ret_branch[l],
                  w_mix_out=w_mix_out[l], norm_xa_g=norm_xa_g[l], xa_wq=xa_wq[l], xa_wo=xa_wo[l],
                  norm_ffn_g=norm_ffn_g[l], router_w=router_w[l], router_b=router_b[l], moe_w_gu=moe_w_gu[l],
                  moe_b_gu=moe_b_gu[l], moe_w_dn=moe_w_dn[l], moe_b_dn=moe_b_dn[l])
        mk_p, mv_p = mem_kv(mem_prompt, norm_mem_g[l], xa_wk[l], xa_wv[l])
        conv0 = jnp.zeros((BATCH, CONV_W - 1, D_LRU), x_prompt.dtype)
        h00 = jnp.zeros((BATCH, D_LRU), jnp.float32)
        S00 = jnp.zeros((BATCH, RET_HEADS, RET_DK, RET_DV), jnp.float32)
        hp, b_p, h_p, S_p = layer(hp, pos_p, conv0, h00, S00, mk_p, mv_p, **lw)
        hs_, b_s, h_s, S_s = layer(hs_, pos_s, state_conv[l], state_lru[l], state_ret[l],
                                   cache_mem_k[l], cache_mem_v[l], **lw)
        p_conv.append(b_p); p_lru.append(h_p); p_ret.append(S_p); p_mk.append(mk_p); p_mv.append(mv_p)
        s_conv.append(b_s); s_lru.append(h_s); s_ret.append(S_s)
    y_prompt = rmsnorm(hp, norm_final_g)
    y_sample = rmsnorm(hs_, norm_final_g)
    return (y_prompt, y_sample, jnp.stack(p_conv), jnp.stack(p_lru), jnp.stack(p_ret), jnp.stack(p_mk),
            jnp.stack(p_mv), jnp.stack(s_conv), jnp.stack(s_lru), jnp.stack(s_ret))
```

```python
import functools
import math

import jax
import jax.numpy as jnp
from jax import lax
from jax.experimental import pallas as pl
from jax.experimental.pallas import tpu as pltpu

EPS = 1e-6
LRU_C = 8.0
ROPE_BASE = 10000.0
PAST_LEN = 16384
RET_CHUNK = 128
CONV_W = 4
TOP_K = 4
SWIGLU_LIMIT = 7.0
SWIGLU_ALPHA = 1.702

V7X_VMEM_LIMIT_BYTES = 56 * 1024 * 1024
BF16 = jnp.bfloat16
F32 = jnp.float32


def _params(*sem):
    return pltpu.CompilerParams(dimension_semantics=sem, vmem_limit_bytes=V7X_VMEM_LIMIT_BYTES)


def _tile(n, pref):
    t = min(n, pref)
    while n % t:
        t //= 2
    return t


def _rms(x, g):
    return x * lax.rsqrt(jnp.mean(x * x, axis=-1, keepdims=True) + EPS) * g


def _rmsnorm_kernel(x_ref, g_ref, o_ref):
    o_ref[...] = _rms(x_ref[...], g_ref[...]).astype(o_ref.dtype)


def rmsnorm(x, g, out_dtype):
    n, d = x.shape
    tm = _tile(n, 512)
    return pl.pallas_call(
        _rmsnorm_kernel,
        grid=(n // tm,),
        in_specs=[pl.BlockSpec((tm, d), lambda i: (i, 0)), pl.BlockSpec((1, d), lambda i: (0, 0))],
        out_specs=pl.BlockSpec((tm, d), lambda i: (i, 0)),
        out_shape=jax.ShapeDtypeStruct((n, d), out_dtype),
        compiler_params=_params("parallel"),
        name="rmsnorm",
    )(x, g.reshape(1, d))


def _add_rmsnorm_kernel(x_ref, y_ref, g_ref, h_ref, hn_ref):
    h = x_ref[...] + y_ref[...]
    h_ref[...] = h
    hn_ref[...] = _rms(h, g_ref[...]).astype(hn_ref.dtype)


def add_rmsnorm(x, y, g):
    n, d = x.shape
    tm = _tile(n, 256)
    row = pl.BlockSpec((tm, d), lambda i: (i, 0))
    return pl.pallas_call(
        _add_rmsnorm_kernel,
        grid=(n // tm,),
        in_specs=[row, row, pl.BlockSpec((1, d), lambda i: (0, 0))],
        out_specs=[row, row],
        out_shape=[jax.ShapeDtypeStruct((n, d), F32), jax.ShapeDtypeStruct((n, d), BF16)],
        compiler_params=_params("parallel"),
        name="add_rmsnorm",
    )(x, y, g.reshape(1, d))


def _matmul_kernel(a_ref, w_ref, o_ref):
    o_ref[...] = jnp.dot(a_ref[...], w_ref[...], preferred_element_type=F32).astype(o_ref.dtype)


def matmul(a, w, out_dtype=F32):
    m, k = a.shape
    _, n = w.shape
    pref = 1024 if k <= 4096 else 512
    tm, tn = _tile(m, pref), _tile(n, pref)
    return pl.pallas_call(
        _matmul_kernel,
        grid=(m // tm, n // tn),
        in_specs=[pl.BlockSpec((tm, k), lambda i, j: (i, 0)), pl.BlockSpec((k, tn), lambda i, j: (0, j))],
        out_specs=pl.BlockSpec((tm, tn), lambda i, j: (i, j)),
        out_shape=jax.ShapeDtypeStruct((m, n), out_dtype),
        compiler_params=_params("parallel", "parallel"),
        name="matmul",
    )(a, w)


def _gelu_tanh(x):
    c = math.sqrt(2.0 / math.pi)
    return x * (0.5 * (1.0 + jnp.tanh(c * (x + 0.044715 * (x * x * x)))))


def _softplus(z):
    return jnp.maximum(z, 0.0) + jnp.log1p(jnp.exp(-jnp.abs(z)))


def _lru_kernel(u_ref, g_ref, buf_ref, h0_ref, cw_ref, cb_ref, wa_ref, ba_ref, wx_ref, bx_ref, lam_ref,
                y_ref, nbuf_ref, ht_ref, ext_s, a_s, b_s, hs_s, h_s, *, tt, nblk, bw):
    t = pl.program_id(1)
    nt = pl.num_programs(1)
    halo = CONV_W - 1

    @pl.when(t == 0)
    def _():
        ext_s[pl.ds(8 - halo, halo), :] = buf_ref[...]
        h_s[...] = h0_ref[...]

    ext_s[pl.ds(8, tt), :] = u_ref[...]
    uc = cb_ref[...] + sum(ext_s[pl.ds(8 - halo + j, tt), :] * cw_ref[pl.ds(j, 1), :] for j in range(CONV_W))
    tail = ext_s[pl.ds(8 + tt - halo, halo), :]
    ext_s[pl.ds(8 - halo, halo), :] = tail

    sp = _softplus(-lam_ref[...])
    for n in range(nblk):
        cs = slice(n * bw, (n + 1) * bw)
        ub = uc[:, cs]
        ubb = ub.astype(BF16)
        r = jax.nn.sigmoid(jnp.dot(ubb, wa_ref[n], preferred_element_type=F32) + ba_ref[:, cs])
        i = jax.nn.sigmoid(jnp.dot(ubb, wx_ref[n], preferred_element_type=F32) + bx_ref[:, cs])
        log_a = -LRU_C * r * sp[:, cs]
        a_s[:, cs] = jnp.exp(log_a)
        one_minus_a2 = -jnp.tanh(log_a) * (jnp.exp(2.0 * log_a) + 1.0)
        b_s[:, cs] = jnp.sqrt(one_minus_a2) * (i * ub)

    def step(s, h):
        h = a_s[pl.ds(s, 1), :] * h + b_s[pl.ds(s, 1), :]
        hs_s[pl.ds(s, 1), :] = h
        return h

    h = lax.fori_loop(0, tt, step, h_s[...], unroll=8)
    h_s[...] = h
    y_ref[...] = (_gelu_tanh(g_ref[...]) * hs_s[...]).astype(y_ref.dtype)

    @pl.when(t == nt - 1)
    def _():
        nbuf_ref[...] = tail
        ht_ref[...] = h


def lru_branch(proj, row0, nb, t_len, conv_buf, h0, conv_w, conv_b, wa, ba, wx, bx, lam):
    c = conv_w.shape[-1]
    nblk, bw, _ = wa.shape
    tt = _tile(t_len, 256)
    nt = t_len // tt
    rb0 = row0 // tt
    vec = pl.BlockSpec((1, c), lambda b, t: (0, 0))
    kern = functools.partial(_lru_kernel, tt=tt, nblk=nblk, bw=bw)
    return pl.pallas_call(
        kern,
        grid=(nb, nt),
        in_specs=[
            pl.BlockSpec((tt, c), lambda b, t: (rb0 + b * nt + t, 0)),
            pl.BlockSpec((tt, c), lambda b, t: (rb0 + b * nt + t, 1)),
            pl.BlockSpec((None, CONV_W - 1, c), lambda b, t: (b, 0, 0)),
            pl.BlockSpec((None, 1, c), lambda b, t: (b, 0, 0)),
            pl.BlockSpec((CONV_W, c), lambda b, t: (0, 0)),
            vec,
            pl.BlockSpec((nblk, bw, bw), lambda b, t: (0, 0, 0)),
            vec,
            pl.BlockSpec((nblk, bw, bw), lambda b, t: (0, 0, 0)),
            vec,
            vec,
        ],
        out_specs=[
            pl.BlockSpec((tt, c), lambda b, t: (b * nt + t, 0)),
            pl.BlockSpec((None, CONV_W - 1, c), lambda b, t: (b, 0, 0)),
            pl.BlockSpec((None, 1, c), lambda b, t: (b, 0, 0)),
        ],
        out_shape=[
            jax.ShapeDtypeStruct((nb * t_len, c), BF16),
            jax.ShapeDtypeStruct((nb, CONV_W - 1, c), F32),
            jax.ShapeDtypeStruct((nb, 1, c), F32),
        ],
        scratch_shapes=[
            pltpu.VMEM((tt + 8, c), F32),
            pltpu.VMEM((tt, c), F32),
            pltpu.VMEM((tt, c), F32),
            pltpu.VMEM((tt, c), F32),
            pltpu.VMEM((1, c), F32),
        ],
        compiler_params=_params("parallel", "arbitrary"),
        name="lru_branch",
    )(proj, proj, conv_buf, h0.reshape(nb, 1, c), conv_w, conv_b.reshape(1, c), wa, ba.reshape(1, c), wx,
      bx.reshape(1, c), lam.reshape(1, c))


def _rotate(x, cos, sin):
    half = x.shape[-1] // 2
    x1, x2 = x[:, :half], x[:, half:]
    return jnp.concatenate([x1 * cos - x2 * sin, x1 * sin + x2 * cos], axis=-1)


def _ret_kernel(q_ref, k_ref, v_ref, g_ref, cos_ref, sin_ref, dec_ref, xi_ref, zeta_ref, gch_ref, s0_ref, ng_ref,
                y_ref, s_ref, *, dk):
    c = pl.program_id(2)

    @pl.when(c == 0)
    def _():
        s_ref[...] = s0_ref[...]

    cos, sin = cos_ref[...], sin_ref[...]
    q = _rotate(q_ref[...], cos, sin)
    k = _rotate(k_ref[...], cos, sin) * (dk ** -0.5)
    vb = v_ref[...].astype(BF16)
    qb = q.astype(BF16)
    s_old = s_ref[...]
    scores = lax.dot_general(qb, k.astype(BF16), (((1,), (1,)), ((), ())), preferred_element_type=F32) * dec_ref[...]
    o = jnp.dot(scores.astype(BF16), vb, preferred_element_type=F32)
    o = o + jnp.dot(qb, s_old.astype(BF16), preferred_element_type=F32) * xi_ref[...]
    kz = (k * zeta_ref[...]).astype(BF16)
    s_ref[...] = gch_ref[...] * s_old + lax.dot_general(kz, vb, (((0,), (0,)), ((), ())), preferred_element_type=F32)
    o = o * lax.rsqrt(jnp.mean(o * o, axis=-1, keepdims=True) + EPS) * ng_ref[...]
    g = g_ref[...]
    y_ref[...] = (g * jax.nn.sigmoid(g) * o).astype(y_ref.dtype)


def retention_branch(proj, row0, nb, t_len, pos0, s0, ret_norm_g, col_q, col_k, col_v, col_g):
    _, nh, dk, dv = s0.shape
    ch = RET_CHUNK if t_len % RET_CHUNK == 0 else t_len
    nc = t_len // ch
    rb0 = row0 // ch
    half = dk // 2
    pos = (pos0 + jnp.arange(t_len, dtype=jnp.int32)).astype(F32)
    inv = ROPE_BASE ** (-jnp.arange(half, dtype=F32) / half)
    ang = pos[:, None] * inv[None, :]
    cos, sin = jnp.cos(ang), jnp.sin(ang)
    log_g = jnp.log1p(-jnp.exp2(-5.0 - jnp.arange(nh, dtype=F32)))
    idx = jnp.arange(ch, dtype=F32)
    diff = idx[:, None] - idx[None, :]
    dec = jnp.where(diff[None] >= 0, jnp.exp(jnp.maximum(diff, 0.0)[None] * log_g[:, None, None]), 0.0)
    xi = jnp.exp((idx + 1.0)[None, :, None] * log_g[:, None, None])
    zeta = jnp.exp((ch - 1.0 - idx)[None, :, None] * log_g[:, None, None])
    gch = jnp.exp(ch * log_g)[:, None, None]

    qb, kb, vb, gb = col_q // dk, col_k // dk, col_v // dv, col_g // dv
    row = lambda b, h, c: rb0 + b * nc + c
    kern = functools.partial(_ret_kernel, dk=dk)
    return pl.pallas_call(
        kern,
        grid=(nb, nh, nc),
        in_specs=[
            pl.BlockSpec((ch, dk), lambda b, h, c: (row(b, h, c), qb + h)),
            pl.BlockSpec((ch, dk), lambda b, h, c: (row(b, h, c), kb + h)),
            pl.BlockSpec((ch, dv), lambda b, h, c: (row(b, h, c), vb + h)),
            pl.BlockSpec((ch, dv), lambda b, h, c: (row(b, h, c), gb + h)),
            pl.BlockSpec((ch, half), lambda b, h, c: (c, 0)),
            pl.BlockSpec((ch, half), lambda b, h, c: (c, 0)),
            pl.BlockSpec((None, ch, ch), lambda b, h, c: (h, 0, 0)),
            pl.BlockSpec((None, ch, 1), lambda b, h, c: (h, 0, 0)),
            pl.BlockSpec((None, ch, 1), lambda b, h, c: (h, 0, 0)),
            pl.BlockSpec((None, 1, 1), lambda b, h, c: (h, 0, 0)),
            pl.BlockSpec((None, None, dk, dv), lambda b, h, c: (b, h, 0, 0)),
            pl.BlockSpec((None, 1, dv), lambda b, h, c: (h, 0, 0)),
        ],
        out_specs=[
            pl.BlockSpec((ch, dv), lambda b, h, c: (b * nc + c, h)),
            pl.BlockSpec((None, None, dk, dv), lambda b, h, c: (b, h, 0, 0)),
        ],
        out_shape=[
            jax.ShapeDtypeStruct((nb * t_len, nh * dv), BF16),
            jax.ShapeDtypeStruct((nb, nh, dk, dv), F32),
        ],
        compiler_params=_params("parallel", "parallel", "arbitrary"),
        name="retention",
    )(proj, proj, proj, proj, cos, sin, dec, xi, zeta, gch, s0, ret_norm_g.reshape(nh, 1, dv))


def _merge_kernel(m1_ref, m2_ref, g1_ref, g2_ref, o_ref):
    o_ref[...] = (jax.nn.sigmoid(g1_ref[...]) * m1_ref[...]
                  + jax.nn.sigmoid(g2_ref[...]) * m2_ref[...]).astype(o_ref.dtype)


def merge(m1, m2, proj, col_g1, col_g2):
    n, d = m1.shape
    tm = _tile(n, 256)
    row = pl.BlockSpec((tm, d), lambda i: (i, 0))
    return pl.pallas_call(
        _merge_kernel,
        grid=(n // tm,),
        in_specs=[row, row, pl.BlockSpec((tm, d), lambda i: (i, col_g1 // d)),
                  pl.BlockSpec((tm, d), lambda i: (i, col_g2 // d))],
        out_specs=row,
        out_shape=jax.ShapeDtypeStruct((n, d), BF16),
        compiler_params=_params("parallel"),
        name="merge",
    )(m1, m2, proj, proj)


def _xattn_kernel(q_ref, k_ref, v_ref, o_ref, *, scale):
    kb = k_ref[...].astype(BF16)
    vb = v_ref[...].astype(BF16)
    s = lax.dot_general(q_ref[...], kb, (((1,), (1,)), ((), ())), preferred_element_type=F32) * scale
    e = jnp.exp(s - jnp.max(s, axis=-1, keepdims=True))
    p = e / jnp.sum(e, axis=-1, keepdims=True)
    o_ref[...] = jnp.dot(p.astype(BF16), vb, preferred_element_type=F32).astype(o_ref.dtype)


def cross_attention(q, row0, nb, t_len, mk, mv, nh):
    d = q.shape[1]
    hd = d // nh
    n_mem = mk.shape[0] // nb
    tq = _tile(t_len, 512)
    nt = t_len // tq
    rb0 = row0 // tq
    kern = functools.partial(_xattn_kernel, scale=hd ** -0.5)
    return pl.pallas_call(
        kern,
        grid=(nb, nh, nt),
        in_specs=[
            pl.BlockSpec((tq, hd), lambda b, h, t: (rb0 + b * nt + t, h)),
            pl.BlockSpec((n_mem, hd), lambda b, h, t: (b, h)),
            pl.BlockSpec((n_mem, hd), lambda b, h, t: (b, h)),
        ],
        out_specs=pl.BlockSpec((tq, hd), lambda b, h, t: (b * nt + t, h)),
        out_shape=jax.ShapeDtypeStruct((nb * t_len, d), BF16),
        compiler_params=_params("parallel", "parallel", "parallel"),
        name="cross_attention",
    )(q, mk, mv)


def _route_kernel(h_ref, xo_ref, g_ref, rw_ref, rb_ref, h2_ref, hn_ref, ti_ref, pr_ref, rk_ref, cnt_ref, cnt_s,
                  *, tm, ne):
    i = pl.program_id(0)

    @pl.when(i == 0)
    def _():
        cnt_s[...] = jnp.zeros_like(cnt_s)

    h2 = h_ref[...] + xo_ref[...]
    h2_ref[...] = h2
    xn = _rms(h2, g_ref[...])
    half = xn.shape[1] // 2
    hi = lax.bitcast_convert_type(xn[:, :half].astype(BF16).astype(F32), jnp.uint32)
    lo = lax.bitcast_convert_type(xn[:, half:].astype(BF16).astype(F32), jnp.uint32)
    hn_ref[...] = hi | (lo >> 16)
    logits = jnp.dot(xn, rw_ref[...], preferred_element_type=F32, precision=lax.Precision.HIGHEST) + rb_ref[...]
    lane = lax.broadcasted_iota(jnp.int32, (tm, ne), 1)
    vals, idxs, hots = [], [], []
    for _ in range(TOP_K):
        m = jnp.max(logits, axis=-1, keepdims=True)
        idx = jnp.min(jnp.where(logits == m, lane, ne), axis=-1, keepdims=True)
        hot = lane == idx
        vals.append(m)
        idxs.append(idx)
        hots.append(hot.astype(F32))
        logits = jnp.where(hot, -jnp.inf, logits)
    es = [jnp.exp(v - vals[0]) for v in vals]
    den = es[0] + es[1] + es[2] + es[3]
    pr_ref[...] = jnp.concatenate([e / den for e in es], axis=-1)
    ti_ref[...] = jnp.concatenate(idxs, axis=-1)
    oh = hots[0] + hots[1] + hots[2] + hots[3]
    r = lax.broadcasted_iota(jnp.int32, (tm, tm), 0)
    cidx = lax.broadcasted_iota(jnp.int32, (tm, tm), 1)
    tri = (r > cidx).astype(BF16)
    base = cnt_s[...] + jnp.dot(tri, oh.astype(BF16), preferred_element_type=F32)
    rk_ref[...] = jnp.concatenate([jnp.sum(hk * base, axis=-1, keepdims=True) for hk in hots],
                                  axis=-1).astype(jnp.int32)
    cnt = cnt_s[...] + jnp.sum(oh, axis=0, keepdims=True)
    cnt_s[...] = cnt
    cnt_ref[...] = cnt.astype(jnp.int32)


def route(h, xo, g, router_w, router_b):
    n, d = h.shape
    ne = router_w.shape[1]
    tm = _tile(n, 256)
    row = pl.BlockSpec((tm, d), lambda i: (i, 0))
    small = pl.BlockSpec((tm, TOP_K), lambda i: (i, 0))
    kern = functools.partial(_route_kernel, tm=tm, ne=ne)
    return pl.pallas_call(
        kern,
        grid=(n // tm,),
        in_specs=[row, row, pl.BlockSpec((1, d), lambda i: (0, 0)), pl.BlockSpec((d, ne), lambda i: (0, 0)),
                  pl.BlockSpec((1, ne), lambda i: (0, 0))],
        out_specs=[row, pl.BlockSpec((tm, d // 2), lambda i: (i, 0)), small, small, small,
                   pl.BlockSpec((1, ne), lambda i: (0, 0))],
        out_shape=[
            jax.ShapeDtypeStruct((n, d), F32),
            jax.ShapeDtypeStruct((n, d // 2), jnp.uint32),
            jax.ShapeDtypeStruct((n, TOP_K), jnp.int32),
            jax.ShapeDtypeStruct((n, TOP_K), F32),
            jax.ShapeDtypeStruct((n, TOP_K), jnp.int32),
            jax.ShapeDtypeStruct((1, ne), jnp.int32),
        ],
        scratch_shapes=[pltpu.VMEM((1, ne), F32)],
        compiler_params=_params("arbitrary"),
        name="route",
    )(h, xo, g.reshape(1, d), router_w, router_b.reshape(1, ne))


def _dispatch_kernel(pos_hbm, x_ref, xs_in, xs_hbm, pos_s, sem_p, sem, *, tm):
    del xs_in
    i = pl.program_id(0)
    cp = pltpu.make_async_copy(pos_hbm.at[i], pos_s, sem_p)
    cp.start()
    cp.wait()

    def row_copy(r, kk):
        return pltpu.make_async_copy(x_ref.at[pl.ds(r, 1)], xs_hbm.at[pl.ds(pos_s[r * TOP_K + kk], 1)], sem)

    def issue(r, carry):
        for kk in range(TOP_K):
            row_copy(r, kk).start()
        return carry

    lax.fori_loop(0, tm, issue, 0)

    def drain(r, carry):
        for kk in range(TOP_K):
            row_copy(r, kk).wait()
        return carry

    lax.fori_loop(0, tm, drain, 0)


def dispatch(x, pos, n_rows):
    n, d = x.shape
    tm = _tile(n, 256)
    kern = functools.partial(_dispatch_kernel, tm=tm)
    return pl.pallas_call(
        kern,
        grid=(n // tm,),
        in_specs=[pl.BlockSpec(memory_space=pl.ANY), pl.BlockSpec((tm, d), lambda i: (i, 0)),
                  pl.BlockSpec(memory_space=pl.ANY)],
        out_specs=pl.BlockSpec(memory_space=pl.ANY),
        out_shape=jax.ShapeDtypeStruct((n_rows, d), x.dtype),
        scratch_shapes=[pltpu.SMEM((tm * TOP_K,), jnp.int32), pltpu.SemaphoreType.DMA, pltpu.SemaphoreType.DMA],
        input_output_aliases={2: 0},
        compiler_params=_params("arbitrary"),
        name="moe_dispatch",
    )(pos.reshape(n // tm, tm * TOP_K), x, jnp.zeros((n_rows, d), x.dtype))


def _gateup_kernel(te_ref, tv_ref, x_ref, w_ref, b_ref, sel_ref, o_ref, wb_s, *, tn):
    t = pl.program_id(1)
    fresh = jnp.logical_or(t == 0, te_ref[t] != te_ref[jnp.maximum(t - 1, 0)])

    @pl.when(jnp.logical_and(fresh, tv_ref[t] == 1))
    def _():
        wb_s[...] = w_ref[...].astype(BF16)

    @pl.when(tv_ref[t] == 1)
    def _():
        xp = x_ref[...]
        half = xp.shape[1]
        x_hi = lax.bitcast_convert_type(xp & jnp.uint32(0xFFFF0000), F32).astype(BF16)
        x_lo = lax.bitcast_convert_type(xp << 16, F32).astype(BF16)
        h = (jnp.dot(x_hi, wb_s[pl.ds(0, half), :], preferred_element_type=F32)
             + jnp.dot(x_lo, wb_s[pl.ds(half, half), :], preferred_element_type=F32) + b_ref[...])
        gate = jnp.minimum(h, SWIGLU_LIMIT)
        glu = gate * jax.nn.sigmoid(SWIGLU_ALPHA * gate)
        up1 = jnp.clip(h, -SWIGLU_LIMIT, SWIGLU_LIMIT) + 1.0
        prod = (glu * pltpu.roll(up1, tn - 1, axis=1)).astype(BF16)
        cw = sel_ref.shape[0]
        parts = [jnp.dot(prod[:, c * cw:(c + 1) * cw], sel_ref[...], preferred_element_type=F32)
                 for c in range(tn // cw)]
        o_ref[...] = jnp.concatenate(parts, axis=-1).astype(o_ref.dtype)

    @pl.when(tv_ref[t] == 0)
    def _():
        o_ref[...] = jnp.zeros_like(o_ref)


def expert_gateup(xs, w_gu, b_gu, tile_expert, tile_valid, tm):
    r = xs.shape[0]
    ne, d, f2 = w_gu.shape
    tn = _tile(f2, 1024)
    cw = min(tn, 256)
    sel = (jnp.arange(cw)[:, None] == 2 * jnp.arange(cw // 2)[None, :]).astype(BF16)
    kern = functools.partial(_gateup_kernel, tn=tn)
    grid_spec = pltpu.PrefetchScalarGridSpec(
        num_scalar_prefetch=2,
        grid=(f2 // tn, r // tm),
        in_specs=[
            pl.BlockSpec((tm, d // 2), lambda j, t, te, tv: (t, 0)),
            pl.BlockSpec((None, d, tn), lambda j, t, te, tv: (te[t], 0, j)),
            pl.BlockSpec((None, 1, tn), lambda j, t, te, tv: (te[t], 0, j)),
            pl.BlockSpec((cw, cw // 2), lambda j, t, te, tv: (0, 0)),
        ],
        out_specs=pl.BlockSpec((tm, tn // 2), lambda j, t, te, tv: (t, j)),
        scratch_shapes=[pltpu.VMEM((d, tn), BF16)],
    )
    return pl.pallas_call(
        kern,
        grid_spec=grid_spec,
        out_shape=jax.ShapeDtypeStruct((r, f2 // 2), BF16),
        compiler_params=_params("arbitrary", "arbitrary"),
        name="moe_gateup",
    )(tile_expert, tile_valid, xs, w_gu, b_gu.reshape(ne, 1, f2), sel)


def _down_kernel(te_ref, tv_ref, a_ref, w_ref, b_ref, o_ref, wb_s):
    t = pl.program_id(1)
    fresh = jnp.logical_or(t == 0, te_ref[t] != te_ref[jnp.maximum(t - 1, 0)])

    @pl.when(jnp.logical_and(fresh, tv_ref[t] == 1))
    def _():
        wb_s[...] = w_ref[...].astype(BF16)

    @pl.when(tv_ref[t] == 1)
    def _():
        o_ref[...] = jnp.dot(a_ref[...], wb_s[...], preferred_element_type=F32) + b_ref[...]

    @pl.when(tv_ref[t] == 0)
    def _():
        o_ref[...] = jnp.zeros_like(o_ref)


def expert_down(act, w_dn, b_dn, tile_expert, tile_valid, tm):
    r, f = act.shape
    ne, _, d = w_dn.shape
    tn = _tile(d, 1024)
    grid_spec = pltpu.PrefetchScalarGridSpec(
        num_scalar_prefetch=2,
        grid=(d // tn, r // tm),
        in_specs=[
            pl.BlockSpec((tm, f), lambda j, t, te, tv: (t, 0)),
            pl.BlockSpec((None, f, tn), lambda j, t, te, tv: (te[t], 0, j)),
            pl.BlockSpec((None, 1, tn), lambda j, t, te, tv: (te[t], 0, j)),
        ],
        out_specs=pl.BlockSpec((tm, tn), lambda j, t, te, tv: (t, j)),
        scratch_shapes=[pltpu.VMEM((f, tn), BF16)],
    )
    return pl.pallas_call(
        _down_kernel,
        grid_spec=grid_spec,
        out_shape=jax.ShapeDtypeStruct((r, d), F32),
        compiler_params=_params("arbitrary", "arbitrary"),
        name="moe_down",
    )(tile_expert, tile_valid, act, w_dn, b_dn.reshape(ne, 1, d))


def _combine_kernel(pos_hbm, ys_hbm, h_ref, p_ref, g_ref, o_ref, pos_s, buf, sem_p, sem, *, tm):
    i = pl.program_id(0)
    cp = pltpu.make_async_copy(pos_hbm.at[i], pos_s, sem_p)
    cp.start()
    cp.wait()

    def row_copy(r, kk):
        return pltpu.make_async_copy(ys_hbm.at[pl.ds(pos_s[r * TOP_K + kk], 1)], buf.at[kk, pl.ds(r, 1)], sem)

    def issue(r, carry):
        for kk in range(TOP_K):
            row_copy(r, kk).start()
        return carry

    lax.fori_loop(0, tm, issue, 0)

    def drain(r, carry):
        for kk in range(TOP_K):
            row_copy(r, kk).wait()
        return carry

    lax.fori_loop(0, tm, drain, 0)
    p = p_ref[...]
    moe = sum(p[:, kk:kk + 1] * buf[kk] for kk in range(TOP_K))
    o_ref[...] = _rms(h_ref[...] + moe, g_ref[...])


def combine(ys, pos, probs, h, g):
    n, d = h.shape
    tm = _tile(n, 128)
    kern = functools.partial(_combine_kernel, tm=tm)
    row = pl.BlockSpec((tm, d), lambda i: (i, 0))
    return pl.pallas_call(
        kern,
        grid=(n // tm,),
        in_specs=[pl.BlockSpec(memory_space=pl.ANY), pl.BlockSpec(memory_space=pl.ANY), row,
                  pl.BlockSpec((tm, TOP_K), lambda i: (i, 0)), pl.BlockSpec((1, d), lambda i: (0, 0))],
        out_specs=row,
        out_shape=jax.ShapeDtypeStruct((n, d), F32),
        scratch_shapes=[pltpu.SMEM((tm * TOP_K,), jnp.int32), pltpu.VMEM((TOP_K, tm, d), F32),
                        pltpu.SemaphoreType.DMA, pltpu.SemaphoreType.DMA],
        compiler_params=_params("arbitrary"),
        name="moe_combine",
    )(pos.reshape(n // tm, tm * TOP_K), ys, h, probs, g.reshape(1, d))


def moe_layer(h, xo, norm_g, router_w, router_b, w_gu, b_gu, w_dn, b_dn, final_g):
    n, d = h.shape
    ne = router_w.shape[1]
    h2, hn, top_i, probs, rank, counts = route(h, xo, norm_g, router_w, router_b)
    tm = 256 if n * TOP_K >= 256 * ne else 8
    counts = counts[0]
    padded = (counts + tm - 1) // tm * tm
    ends = jnp.cumsum(padded)
    offs = ends - padded
    pos = offs[top_i] + rank
    n_rows = -(-(n * TOP_K) // tm) * tm + ne * tm
    tile_start = jnp.arange(n_rows // tm, dtype=jnp.int32) * tm
    te = jnp.searchsorted(ends, tile_start, side="right").astype(jnp.int32)
    tile_valid = (tile_start < ends[-1]).astype(jnp.int32)
    last_e = jnp.searchsorted(ends, ends[-1] - 1, side="right").astype(jnp.int32)
    tile_expert = jnp.where(tile_valid == 1, te, last_e)
    xs = dispatch(hn, pos, n_rows)
    act = expert_gateup(xs, w_gu, b_gu, tile_expert, tile_valid, tm)
    ys = expert_down(act, w_dn, b_dn, tile_expert, tile_valid, tm)
    return combine(ys, pos, probs, h2, final_g)


def kernel(x_prompt, x_sample, mem_prompt, state_conv, state_lru, state_ret, cache_mem_k, cache_mem_v, norm_mix_g, w_in, conv_w, conv_b, lru_wa, lru_ba, lru_wx, lru_bx, lru_lambda, ret_norm_g, w_lru_branch, w_ret_branch, w_mix_out, norm_xa_g, norm_mem_g, xa_wq, xa_wk, xa_wv, xa_wo, norm_ffn_g, router_w, router_b, moe_w_gu, moe_b_gu, moe_w_dn, moe_b_dn, norm_final_g):
    depth = w_in.shape[0]
    assert depth == 1, "single-layer trunk"
    bp, tp, d = x_prompt.shape
    bs, ts, _ = x_sample.shape
    np_, ns = bp * tp, bs * ts
    n_mem = mem_prompt.shape[1]
    _, _, nh, dk, dv = state_ret.shape
    xh = cache_mem_k.shape[3]
    c_lru = conv_w.shape[-1]
    l = 0
    sizes = (c_lru, c_lru, nh * dk, nh * dk, nh * dv, nh * dv, d, d)
    cols = [0]
    for s in sizes:
        cols.append(cols[-1] + s)
    c_u, c_gl, c_q, c_k, c_v, c_gr, c_g1, c_g2 = cols[:8]
    assert c_u == 0 and c_gl == c_lru

    x = jnp.concatenate([x_prompt.reshape(np_, d), x_sample.reshape(ns, d)], axis=0)
    xn = rmsnorm(x, norm_mix_g[l], BF16)
    proj = matmul(xn, w_in[l].astype(BF16))

    wa, wx = lru_wa[l].astype(BF16), lru_wx[l].astype(BF16)
    lru_args = (conv_w[l], conv_b[l], wa, lru_ba[l], wx, lru_bx[l], lru_lambda[l])
    yl_p, pconv, plru = lru_branch(proj, 0, bp, tp, jnp.zeros((bp, CONV_W - 1, c_lru), F32),
                                   jnp.zeros((bp, c_lru), F32), *lru_args)
    yl_s, sconv, slru = lru_branch(proj, np_, bs, ts, state_conv[l], state_lru[l], *lru_args)
    yr_p, pret = retention_branch(proj, 0, bp, tp, 0, jnp.zeros((bp, nh, dk, dv), F32), ret_norm_g[l],
                                  c_q, c_k, c_v, c_gr)
    yr_s, sret = retention_branch(proj, np_, bs, ts, PAST_LEN, state_ret[l], ret_norm_g[l], c_q, c_k, c_v, c_gr)
    y_lru = jnp.concatenate([yl_p, yl_s], axis=0)
    y_ret = jnp.concatenate([yr_p, yr_s], axis=0)
    m1 = matmul(y_lru, w_lru_branch[l].astype(BF16))
    m2 = matmul(y_ret, w_ret_branch[l].astype(BF16))
    merged = merge(m1, m2, proj, c_g1, c_g2)
    mix = matmul(merged, w_mix_out[l].astype(BF16))
    h, hn = add_rmsnorm(x, mix, norm_xa_g[l])

    mn = rmsnorm(mem_prompt.reshape(bp * n_mem, d), norm_mem_g[l], BF16)
    mk_p = matmul(mn, xa_wk[l].astype(BF16))
    mv_p = matmul(mn, xa_wv[l].astype(BF16))
    q = matmul(hn, xa_wq[l].astype(BF16), BF16)
    o_p = cross_attention(q, 0, bp, tp, mk_p, mv_p, xh)
    o_s = cross_attention(q, np_, bs, ts, cache_mem_k[l].reshape(bs * n_mem, d),
                          cache_mem_v[l].reshape(bs * n_mem, d), xh)
    xo = matmul(jnp.concatenate([o_p, o_s], axis=0), xa_wo[l].astype(BF16))

    y = moe_layer(h, xo, norm_ffn_g[l], router_w[l], router_b[l], moe_w_gu[l], moe_b_gu[l], moe_w_dn[l],
                  moe_b_dn[l], norm_final_g)

    hd = d // xh
    return (y[:np_].reshape(bp, tp, d), y[np_:].reshape(bs, ts, d),
            pconv[None], plru.reshape(1, bp, c_lru), pret[None],
            mk_p.reshape(1, bp, n_mem, xh, hd), mv_p.reshape(1, bp, n_mem, xh, hd),
            sconv[None], slru.reshape(1, bs, c_lru), sret[None])
```

```python
import functools
import math

import jax
import jax.numpy as jnp
from jax import lax
from jax.experimental import pallas as pl
from jax.experimental.pallas import tpu as pltpu

EPS = 1e-6
LRU_C = 8.0
ROPE_BASE = 10000.0
PAST_LEN = 16384
RET_CHUNK = 128
CONV_W = 4
TOP_K = 4
SWIGLU_LIMIT = 7.0
SWIGLU_ALPHA = 1.702

V7X_VMEM_LIMIT_BYTES = 56 * 1024 * 1024
V7X_MXU_COLS = 256
BF16 = jnp.bfloat16
F32 = jnp.float32


def _params(*sem):
    return pltpu.CompilerParams(dimension_semantics=sem, vmem_limit_bytes=V7X_VMEM_LIMIT_BYTES)


def _tile(n, pref):
    t = min(n, pref)
    while n % t:
        t //= 2
    return t


def _rms(x, g):
    return x * lax.rsqrt(jnp.mean(x * x, axis=-1, keepdims=True) + EPS) * g


def _rmsnorm_kernel(x_ref, g_ref, o_ref):
    o_ref[...] = _rms(x_ref[...], g_ref[...]).astype(o_ref.dtype)


def rmsnorm(x, g, out_dtype):
    n, d = x.shape
    tm = _tile(n, 512)
    return pl.pallas_call(
        _rmsnorm_kernel,
        grid=(n // tm,),
        in_specs=[pl.BlockSpec((tm, d), lambda i: (i, 0)), pl.BlockSpec((1, d), lambda i: (0, 0))],
        out_specs=pl.BlockSpec((tm, d), lambda i: (i, 0)),
        out_shape=jax.ShapeDtypeStruct((n, d), out_dtype),
        compiler_params=_params("parallel"),
        name="rmsnorm",
    )(x, g.reshape(1, d))


def _add_rmsnorm_kernel(x_ref, y_ref, g_ref, h_ref, hn_ref):
    h = x_ref[...] + y_ref[...]
    h_ref[...] = h
    hn_ref[...] = _rms(h, g_ref[...]).astype(hn_ref.dtype)


def add_rmsnorm(x, y, g):
    n, d = x.shape
    tm = _tile(n, 256)
    row = pl.BlockSpec((tm, d), lambda i: (i, 0))
    return pl.pallas_call(
        _add_rmsnorm_kernel,
        grid=(n // tm,),
        in_specs=[row, row, pl.BlockSpec((1, d), lambda i: (0, 0))],
        out_specs=[row, row],
        out_shape=[jax.ShapeDtypeStruct((n, d), F32), jax.ShapeDtypeStruct((n, d), BF16)],
        compiler_params=_params("parallel"),
        name="add_rmsnorm",
    )(x, y, g.reshape(1, d))


def _swiglu_pairs(h, sel):
    gate = jnp.minimum(h, SWIGLU_LIMIT)
    glu = gate * jax.nn.sigmoid(SWIGLU_ALPHA * gate)
    up1 = jnp.clip(h, -SWIGLU_LIMIT, SWIGLU_LIMIT) + 1.0
    prod = (glu * pltpu.roll(up1, h.shape[1] - 1, axis=1)).astype(BF16)
    return jnp.dot(prod, sel, preferred_element_type=F32)


def _gmm_kernel(te_ref, tv_ref, first_ref, nxt_ref, lastg_ref, a_ref, w_hbm, *rest,
                tn, nj, cw, packed, has_bias, swiglu):
    rest = list(rest)
    b_ref = rest.pop(0) if has_bias else None
    sel_ref = rest.pop(0) if swiglu else None
    o_ref, wf_s, wb_s, sem = rest
    j = pl.program_id(0)
    t = pl.program_id(1)

    def w_copy(e, jj):
        return pltpu.make_async_copy(w_hbm.at[e, :, pl.ds(pl.multiple_of(jj * tn, tn), tn)], wf_s, sem)

    @pl.when(first_ref[t] == 1)
    def _():
        @pl.when(jnp.logical_and(j == 0, t == 0))
        def _():
            w_copy(te_ref[0], 0).start()

        w_copy(te_ref[t], j).wait()
        wb_s[...] = wf_s[...].astype(BF16)
        lastg = lastg_ref[t] == 1

        @pl.when(jnp.logical_not(jnp.logical_and(lastg, j == nj - 1)))
        def _():
            w_copy(nxt_ref[t], jnp.where(lastg, j + 1, j)).start()

    @pl.when(tv_ref[t] == 1)
    def _():
        if packed:
            xp = a_ref[...]
            half = xp.shape[1]
            x_hi = lax.bitcast_convert_type(xp & jnp.uint32(0xFFFF0000), F32).astype(BF16)
            x_lo = lax.bitcast_convert_type(xp << 16, F32).astype(BF16)
        else:
            x = a_ref[...]
        ow = cw // 2 if swiglu else cw
        for c in range(tn // cw):
            cols = pl.ds(c * cw, cw)
            if packed:
                h = (jnp.dot(x_hi, wb_s[pl.ds(0, half), cols], preferred_element_type=F32)
                     + jnp.dot(x_lo, wb_s[pl.ds(half, half), cols], preferred_element_type=F32))
            else:
                h = jnp.dot(x, wb_s[:, cols], preferred_element_type=F32)
            if has_bias:
                h = h + b_ref[:, cols]
            if swiglu:
                h = _swiglu_pairs(h, sel_ref[...])
            o_ref[:, pl.ds(c * ow, ow)] = h.astype(o_ref.dtype)

    @pl.when(tv_ref[t] == 0)
    def _():
        o_ref[...] = jnp.zeros_like(o_ref)


def _dense_meta(n_tiles):
    z = jnp.zeros((n_tiles,), jnp.int32)
    one = jnp.ones((n_tiles,), jnp.int32)
    return z, one, z.at[0].set(1), z, one


def gmm(a, w, meta, tm, tn, *, bias=None, swiglu=False, packed=False, out_dtype=F32):
    r = a.shape[0]
    g, k, n = w.shape
    nj = n // tn
    cw = min(tn, 2 * V7X_MXU_COLS)
    ow_total = n // 2 if swiglu else n
    otn = tn // 2 if swiglu else tn
    in_specs = [
        pl.BlockSpec((tm, a.shape[1]), lambda j, t, *_: (t, 0)),
        pl.BlockSpec(memory_space=pl.ANY),
    ]
    args = [a, w]
    if bias is not None:
        in_specs.append(pl.BlockSpec((None, 1, tn), lambda j, t, te, *_: (te[t], 0, j)))
        args.append(bias.reshape(g, 1, n))
    if swiglu:
        in_specs.append(pl.BlockSpec((cw, cw // 2), lambda j, t, *_: (0, 0)))
        args.append((jnp.arange(cw)[:, None] == 2 * jnp.arange(cw // 2)[None, :]).astype(BF16))
    kern = functools.partial(_gmm_kernel, tn=tn, nj=nj, cw=cw, packed=packed, has_bias=bias is not None,
                             swiglu=swiglu)
    grid_spec = pltpu.PrefetchScalarGridSpec(
        num_scalar_prefetch=5,
        grid=(nj, r // tm),
        in_specs=in_specs,
        out_specs=pl.BlockSpec((tm, otn), lambda j, t, *_: (t, j)),
        scratch_shapes=[pltpu.VMEM((k, tn), F32), pltpu.VMEM((k, tn), BF16), pltpu.SemaphoreType.DMA],
    )
    return pl.pallas_call(
        kern,
        grid_spec=grid_spec,
        out_shape=jax.ShapeDtypeStruct((r, ow_total), out_dtype),
        compiler_params=_params("arbitrary", "arbitrary"),
        name="gmm_swiglu" if swiglu else "gmm",
    )(*meta, *args)


def dense(a, w, out_dtype=F32):
    m, k = a.shape
    pref = 1024 if k <= 4096 else 512
    tm, tn = _tile(m, pref), _tile(w.shape[2], pref)
    return gmm(a, w, _dense_meta(m // tm), tm, tn, out_dtype=out_dtype)


def _gelu_tanh(x):
    c = math.sqrt(2.0 / math.pi)
    return x * (0.5 * (1.0 + jnp.tanh(c * (x + 0.044715 * (x * x * x)))))


def _softplus(z):
    return jnp.maximum(z, 0.0) + jnp.log1p(jnp.exp(-jnp.abs(z)))


def _lru_kernel(u_ref, g_ref, buf_ref, h0_ref, cw_ref, cb_ref, wa_ref, ba_ref, wx_ref, bx_ref, lam_ref, yin_ref,
                y_ref, nbuf_ref, ht_ref, ext_s, a_s, b_s, hs_s, h_s, *, tt, nblk, bw):
    del yin_ref
    t = pl.program_id(1)
    nt = pl.num_programs(1)
    halo = CONV_W - 1

    @pl.when(t == 0)
    def _():
        ext_s[pl.ds(8 - halo, halo), :] = buf_ref[...]
        h_s[...] = h0_ref[...]

    ext_s[pl.ds(8, tt), :] = u_ref[...]
    uc = cb_ref[...] + sum(ext_s[pl.ds(8 - halo + j, tt), :] * cw_ref[pl.ds(j, 1), :] for j in range(CONV_W))
    tail = ext_s[pl.ds(8 + tt - halo, halo), :]
    ext_s[pl.ds(8 - halo, halo), :] = tail

    sp = _softplus(-lam_ref[...])
    for n in range(nblk):
        cs = slice(n * bw, (n + 1) * bw)
        ub = uc[:, cs]
        ubb = ub.astype(BF16)
        r = jax.nn.sigmoid(jnp.dot(ubb, wa_ref[n].astype(BF16), preferred_element_type=F32) + ba_ref[:, cs])
        i = jax.nn.sigmoid(jnp.dot(ubb, wx_ref[n].astype(BF16), preferred_element_type=F32) + bx_ref[:, cs])
        log_a = -LRU_C * r * sp[:, cs]
        a_s[:, cs] = jnp.exp(log_a)
        one_minus_a2 = -jnp.tanh(log_a) * (jnp.exp(2.0 * log_a) + 1.0)
        b_s[:, cs] = jnp.sqrt(one_minus_a2) * (i * ub)

    def step(s, h):
        h = a_s[pl.ds(s, 1), :] * h + b_s[pl.ds(s, 1), :]
        hs_s[pl.ds(s, 1), :] = h
        return h

    h = lax.fori_loop(0, tt, step, h_s[...], unroll=8)
    h_s[...] = h
    y_ref[...] = (_gelu_tanh(g_ref[...]) * hs_s[...]).astype(y_ref.dtype)

    @pl.when(t == nt - 1)
    def _():
        nbuf_ref[...] = tail
        ht_ref[...] = h


def lru_branch(proj, y_all, row0, nb, t_len, conv_buf, h0, conv_w, conv_b, wa, ba, wx, bx, lam):
    c = conv_w.shape[-1]
    nblk, bw, _ = wa.shape
    tt = _tile(t_len, 256)
    nt = t_len // tt
    rb0 = row0 // tt
    vec = pl.BlockSpec((1, c), lambda b, t: (0, 0))
    kern = functools.partial(_lru_kernel, tt=tt, nblk=nblk, bw=bw)
    return pl.pallas_call(
        kern,
        grid=(nb, nt),
        in_specs=[
            pl.BlockSpec((tt, c), lambda b, t: (rb0 + b * nt + t, 0)),
            pl.BlockSpec((tt, c), lambda b, t: (rb0 + b * nt + t, 1)),
            pl.BlockSpec((None, CONV_W - 1, c), lambda b, t: (b, 0, 0)),
            pl.BlockSpec((None, 1, c), lambda b, t: (b, 0, 0)),
            pl.BlockSpec((CONV_W, c), lambda b, t: (0, 0)),
            vec,
            pl.BlockSpec((nblk, bw, bw), lambda b, t: (0, 0, 0)),
            vec,
            pl.BlockSpec((nblk, bw, bw), lambda b, t: (0, 0, 0)),
            vec,
            vec,
            pl.BlockSpec(memory_space=pl.ANY),
        ],
        out_specs=[
            pl.BlockSpec((tt, c), lambda b, t: (rb0 + b * nt + t, 0)),
            pl.BlockSpec((None, CONV_W - 1, c), lambda b, t: (b, 0, 0)),
            pl.BlockSpec((None, 1, c), lambda b, t: (b, 0, 0)),
        ],
        out_shape=[
            jax.ShapeDtypeStruct(y_all.shape, y_all.dtype),
            jax.ShapeDtypeStruct((nb, CONV_W - 1, c), F32),
            jax.ShapeDtypeStruct((nb, 1, c), F32),
        ],
        scratch_shapes=[
            pltpu.VMEM((tt + 8, c), F32),
            pltpu.VMEM((tt, c), F32),
            pltpu.VMEM((tt, c), F32),
            pltpu.VMEM((tt, c), F32),
            pltpu.VMEM((1, c), F32),
        ],
        input_output_aliases={11: 0},
        compiler_params=_params("parallel", "arbitrary"),
        name="lru_branch",
    )(proj, proj, conv_buf, h0.reshape(nb, 1, c), conv_w, conv_b.reshape(1, c), wa, ba.reshape(1, c), wx,
      bx.reshape(1, c), lam.reshape(1, c), y_all)


def _rotate(x, cos, sin):
    half = x.shape[-1] // 2
    x1, x2 = x[:, :half], x[:, half:]
    return jnp.concatenate([x1 * cos - x2 * sin, x1 * sin + x2 * cos], axis=-1)


def _ret_kernel(q_ref, k_ref, v_ref, g_ref, cos_ref, sin_ref, dec_ref, xi_ref, zeta_ref, gch_ref, s0_ref, ng_ref,
                yin_ref, y_ref, s_ref, *, dk, dv, hb):
    del yin_ref
    c = pl.program_id(2)

    @pl.when(c == 0)
    def _():
        s_ref[...] = s0_ref[...]

    cos, sin = cos_ref[...], sin_ref[...]
    for hh in range(hb):
        kc, vc = pl.ds(hh * dk, dk), pl.ds(hh * dv, dv)
        q = _rotate(q_ref[:, kc], cos, sin)
        k = _rotate(k_ref[:, kc], cos, sin) * (dk ** -0.5)
        vb = v_ref[:, vc].astype(BF16)
        qb = q.astype(BF16)
        s_old = s_ref[hh]
        scores = lax.dot_general(qb, k.astype(BF16), (((1,), (1,)), ((), ())),
                                 preferred_element_type=F32) * dec_ref[hh]
        o = jnp.dot(scores.astype(BF16), vb, preferred_element_type=F32)
        o = o + jnp.dot(qb, s_old.astype(BF16), preferred_element_type=F32) * xi_ref[hh]
        kz = (k * zeta_ref[hh]).astype(BF16)
        s_ref[hh] = gch_ref[hh] * s_old + lax.dot_general(kz, vb, (((0,), (0,)), ((), ())),
                                                          preferred_element_type=F32)
        o = o * lax.rsqrt(jnp.mean(o * o, axis=-1, keepdims=True) + EPS) * ng_ref[hh]
        g = g_ref[:, vc]
        y_ref[:, vc] = (g * jax.nn.sigmoid(g) * o).astype(y_ref.dtype)


def retention_branch(proj, y_all, row0, nb, t_len, pos0, s0, ret_norm_g, col_q, col_k, col_v, col_g):
    _, nh, dk, dv = s0.shape
    ch = RET_CHUNK if t_len % RET_CHUNK == 0 else t_len
    nc = t_len // ch
    rb0 = row0 // ch
    half = dk // 2
    hb = _tile(nh, 4)
    pos = (pos0 + jnp.arange(t_len, dtype=jnp.int32)).astype(F32)
    inv = ROPE_BASE ** (-jnp.arange(half, dtype=F32) / half)
    ang = pos[:, None] * inv[None, :]
    cos, sin = jnp.cos(ang), jnp.sin(ang)
    log_g = jnp.log1p(-jnp.exp2(-5.0 - jnp.arange(nh, dtype=F32)))
    idx = jnp.arange(ch, dtype=F32)
    diff = idx[:, None] - idx[None, :]
    dec = jnp.where(diff[None] >= 0, jnp.exp(jnp.maximum(diff, 0.0)[None] * log_g[:, None, None]), 0.0)
    xi = jnp.exp((idx + 1.0)[None, :, None] * log_g[:, None, None])
    zeta = jnp.exp((ch - 1.0 - idx)[None, :, None] * log_g[:, None, None])
    gch = jnp.exp(ch * log_g)[:, None, None]

    qb, kb, vb, gb = col_q // (hb * dk), col_k // (hb * dk), col_v // (hb * dv), col_g // (hb * dv)
    row = lambda b, h, c: rb0 + b * nc + c
    kern = functools.partial(_ret_kernel, dk=dk, dv=dv, hb=hb)
    return pl.pallas_call(
        kern,
        grid=(nb, nh // hb, nc),
        in_specs=[
            pl.BlockSpec((ch, hb * dk), lambda b, h, c: (row(b, h, c), qb + h)),
            pl.BlockSpec((ch, hb * dk), lambda b, h, c: (row(b, h, c), kb + h)),
            pl.BlockSpec((ch, hb * dv), lambda b, h, c: (row(b, h, c), vb + h)),
            pl.BlockSpec((ch, hb * dv), lambda b, h, c: (row(b, h, c), gb + h)),
            pl.BlockSpec((ch, half), lambda b, h, c: (c, 0)),
            pl.BlockSpec((ch, half), lambda b, h, c: (c, 0)),
            pl.BlockSpec((hb, ch, ch), lambda b, h, c: (h, 0, 0)),
            pl.BlockSpec((hb, ch, 1), lambda b, h, c: (h, 0, 0)),
            pl.BlockSpec((hb, ch, 1), lambda b, h, c: (h, 0, 0)),
            pl.BlockSpec((hb, 1, 1), lambda b, h, c: (h, 0, 0)),
            pl.BlockSpec((None, hb, dk, dv), lambda b, h, c: (b, h, 0, 0)),
            pl.BlockSpec((hb, 1, dv), lambda b, h, c: (h, 0, 0)),
            pl.BlockSpec(memory_space=pl.ANY),
        ],
        out_specs=[
            pl.BlockSpec((ch, hb * dv), lambda b, h, c: (row(b, h, c), h)),
            pl.BlockSpec((None, hb, dk, dv), lambda b, h, c: (b, h, 0, 0)),
        ],
        out_shape=[
            jax.ShapeDtypeStruct(y_all.shape, y_all.dtype),
            jax.ShapeDtypeStruct((nb, nh, dk, dv), F32),
        ],
        input_output_aliases={12: 0},
        compiler_params=_params("parallel", "parallel", "arbitrary"),
        name="retention",
    )(proj, proj, proj, proj, cos, sin, dec, xi, zeta, gch, s0, ret_norm_g.reshape(nh, 1, dv), y_all)


def _merge_kernel(m1_ref, m2_ref, g1_ref, g2_ref, o_ref):
    o_ref[...] = (jax.nn.sigmoid(g1_ref[...]) * m1_ref[...]
                  + jax.nn.sigmoid(g2_ref[...]) * m2_ref[...]).astype(o_ref.dtype)


def merge(m1, m2, proj, col_g1, col_g2):
    n, d = m1.shape
    tm = _tile(n, 256)
    row = pl.BlockSpec((tm, d), lambda i: (i, 0))
    return pl.pallas_call(
        _merge_kernel,
        grid=(n // tm,),
        in_specs=[row, row, pl.BlockSpec((tm, d), lambda i: (i, col_g1 // d)),
                  pl.BlockSpec((tm, d), lambda i: (i, col_g2 // d))],
        out_specs=row,
        out_shape=jax.ShapeDtypeStruct((n, d), BF16),
        compiler_params=_params("parallel"),
        name="merge",
    )(m1, m2, proj, proj)


def _xattn_kernel(q_ref, k_ref, v_ref, oin_ref, o_ref, *, scale, nh, hd):
    del oin_ref
    for h in range(nh):
        cols = pl.ds(h * hd, hd)
        kb = k_ref[:, cols].astype(BF16)
        vb = v_ref[:, cols].astype(BF16)
        s = lax.dot_general(q_ref[:, cols], kb, (((1,), (1,)), ((), ())), preferred_element_type=F32) * scale
        e = jnp.exp(s - jnp.max(s, axis=-1, keepdims=True))
        p = e / jnp.sum(e, axis=-1, keepdims=True)
        o_ref[:, cols] = jnp.dot(p.astype(BF16), vb, preferred_element_type=F32).astype(o_ref.dtype)


def cross_attention(q, o_all, row0, nb, t_len, mk, mv, nh):
    d = q.shape[1]
    hd = d // nh
    n_mem = mk.shape[0] // nb
    tq = _tile(t_len, 512)
    nt = t_len // tq
    rb0 = row0 // tq
    kern = functools.partial(_xattn_kernel, scale=hd ** -0.5, nh=nh, hd=hd)
    return pl.pallas_call(
        kern,
        grid=(nb, nt),
        in_specs=[
            pl.BlockSpec((tq, d), lambda b, t: (rb0 + b * nt + t, 0)),
            pl.BlockSpec((n_mem, d), lambda b, t: (b, 0)),
            pl.BlockSpec((n_mem, d), lambda b, t: (b, 0)),
            pl.BlockSpec(memory_space=pl.ANY),
        ],
        out_specs=pl.BlockSpec((tq, d), lambda b, t: (rb0 + b * nt + t, 0)),
        out_shape=jax.ShapeDtypeStruct(o_all.shape, o_all.dtype),
        input_output_aliases={3: 0},
        compiler_params=_params("parallel", "parallel"),
        name="cross_attention",
    )(q, mk, mv, o_all)


def _route_kernel(h_ref, xo_ref, g_ref, rw_ref, rb_ref, h2_ref, hn_ref, ti_ref, pr_ref, rk_ref, cnt_ref, cnt_s,
                  *, tm, ne):
    i = pl.program_id(0)

    @pl.when(i == 0)
    def _():
        cnt_s[...] = jnp.zeros_like(cnt_s)

    h2 = h_ref[...] + xo_ref[...]
    h2_ref[...] = h2
    xn = _rms(h2, g_ref[...])
    half = xn.shape[1] // 2
    hi = lax.bitcast_convert_type(xn[:, :half].astype(BF16).astype(F32), jnp.uint32)
    lo = lax.bitcast_convert_type(xn[:, half:].astype(BF16).astype(F32), jnp.uint32)
    hn_ref[...] = hi | (lo >> 16)
    logits = jnp.dot(xn, rw_ref[...], preferred_element_type=F32, precision=lax.Precision.HIGHEST) + rb_ref[...]
    lane = lax.broadcasted_iota(jnp.int32, (tm, ne), 1)
    vals, idxs, hots = [], [], []
    for _ in range(TOP_K):
        m = jnp.max(logits, axis=-1, keepdims=True)
        idx = jnp.min(jnp.where(logits == m, lane, ne), axis=-1, keepdims=True)
        hot = lane == idx
        vals.append(m)
        idxs.append(idx)
        hots.append(hot.astype(F32))
        logits = jnp.where(hot, -jnp.inf, logits)
    es = [jnp.exp(v - vals[0]) for v in vals]
    den = es[0] + es[1] + es[2] + es[3]
    pr_ref[...] = jnp.concatenate([e / den for e in es], axis=-1)
    ti_ref[...] = jnp.concatenate(idxs, axis=-1)
    oh = hots[0] + hots[1] + hots[2] + hots[3]
    r = lax.broadcasted_iota(jnp.int32, (tm, tm), 0)
    cidx = lax.broadcasted_iota(jnp.int32, (tm, tm), 1)
    tri = (r > cidx).astype(BF16)
    base = cnt_s[...] + jnp.dot(tri, oh.astype(BF16), preferred_element_type=F32)
    rk_ref[...] = jnp.concatenate([jnp.sum(hk * base, axis=-1, keepdims=True) for hk in hots],
                                  axis=-1).astype(jnp.int32)
    cnt = cnt_s[...] + jnp.sum(oh, axis=0, keepdims=True)
    cnt_s[...] = cnt
    cnt_ref[...] = cnt.astype(jnp.int32)


def route(h, xo, g, router_w, router_b):
    n, d = h.shape
    ne = router_w.shape[1]
    tm = _tile(n, 256)
    row = pl.BlockSpec((tm, d), lambda i: (i, 0))
    small = pl.BlockSpec((tm, TOP_K), lambda i: (i, 0))
    kern = functools.partial(_route_kernel, tm=tm, ne=ne)
    return pl.pallas_call(
        kern,
        grid=(n // tm,),
        in_specs=[row, row, pl.BlockSpec((1, d), lambda i: (0, 0)), pl.BlockSpec((d, ne), lambda i: (0, 0)),
                  pl.BlockSpec((1, ne), lambda i: (0, 0))],
        out_specs=[row, pl.BlockSpec((tm, d // 2), lambda i: (i, 0)), small, small, small,
                   pl.BlockSpec((1, ne), lambda i: (0, 0))],
        out_shape=[
            jax.ShapeDtypeStruct((n, d), F32),
            jax.ShapeDtypeStruct((n, d // 2), jnp.uint32),
            jax.ShapeDtypeStruct((n, TOP_K), jnp.int32),
            jax.ShapeDtypeStruct((n, TOP_K), F32),
            jax.ShapeDtypeStruct((n, TOP_K), jnp.int32),
            jax.ShapeDtypeStruct((1, ne), jnp.int32),
        ],
        scratch_shapes=[pltpu.VMEM((1, ne), F32)],
        compiler_params=_params("arbitrary"),
        name="route",
    )(h, xo, g.reshape(1, d), router_w, router_b.reshape(1, ne))


def _dispatch_kernel(pos_hbm, x_ref, xs_in, xs_hbm, pos_s, sem_p, sem, *, tm):
    del xs_in
    i = pl.program_id(0)
    cp = pltpu.make_async_copy(pos_hbm.at[i], pos_s, sem_p)
    cp.start()
    cp.wait()

    def row_copy(r, kk):
        return pltpu.make_async_copy(x_ref.at[pl.ds(r, 1)], xs_hbm.at[pl.ds(pos_s[r * TOP_K + kk], 1)], sem)

    def issue(r, carry):
        for kk in range(TOP_K):
            row_copy(r, kk).start()
        return carry

    lax.fori_loop(0, tm, issue, 0)

    def drain(r, carry):
        for kk in range(TOP_K):
            row_copy(r, kk).wait()
        return carry

    lax.fori_loop(0, tm, drain, 0)


def dispatch(x, pos, n_rows):
    n, d = x.shape
    tm = _tile(n, 256)
    kern = functools.partial(_dispatch_kernel, tm=tm)
    return pl.pallas_call(
        kern,
        grid=(n // tm,),
        in_specs=[pl.BlockSpec(memory_space=pl.ANY), pl.BlockSpec((tm, d), lambda i: (i, 0)),
                  pl.BlockSpec(memory_space=pl.ANY)],
        out_specs=pl.BlockSpec(memory_space=pl.ANY),
        out_shape=jax.ShapeDtypeStruct((n_rows, d), x.dtype),
        scratch_shapes=[pltpu.SMEM((tm * TOP_K,), jnp.int32), pltpu.SemaphoreType.DMA, pltpu.SemaphoreType.DMA],
        input_output_aliases={2: 0},
        compiler_params=_params("arbitrary"),
        name="moe_dispatch",
    )(pos.reshape(n // tm, tm * TOP_K), x, jnp.zeros((n_rows, d), x.dtype))


def _combine_kernel(pos_hbm, ys_hbm, h_ref, p_ref, g_ref, o_ref, pos_s, buf, sem_p, sem, *, tm):
    i = pl.program_id(0)
    cp = pltpu.make_async_copy(pos_hbm.at[i], pos_s, sem_p)
    cp.start()
    cp.wait()

    def row_copy(r, kk):
        return pltpu.make_async_copy(ys_hbm.at[pl.ds(pos_s[r * TOP_K + kk], 1)], buf.at[kk, pl.ds(r, 1)], sem)

    def issue(r, carry):
        for kk in range(TOP_K):
            row_copy(r, kk).start()
        return carry

    lax.fori_loop(0, tm, issue, 0)

    def drain(r, carry):
        for kk in range(TOP_K):
            row_copy(r, kk).wait()
        return carry

    lax.fori_loop(0, tm, drain, 0)
    p = p_ref[...]
    moe = sum(p[:, kk:kk + 1] * buf[kk] for kk in range(TOP_K))
    o_ref[...] = _rms(h_ref[...] + moe, g_ref[...])


def combine(ys, pos, probs, h, g):
    n, d = h.shape
    tm = _tile(n, 128)
    kern = functools.partial(_combine_kernel, tm=tm)
    row = pl.BlockSpec((tm, d), lambda i: (i, 0))
    return pl.pallas_call(
        kern,
        grid=(n // tm,),
        in_specs=[pl.BlockSpec(memory_space=pl.ANY), pl.BlockSpec(memory_space=pl.ANY), row,
                  pl.BlockSpec((tm, TOP_K), lambda i: (i, 0)), pl.BlockSpec((1, d), lambda i: (0, 0))],
        out_specs=row,
        out_shape=jax.ShapeDtypeStruct((n, d), F32),
        scratch_shapes=[pltpu.SMEM((tm * TOP_K,), jnp.int32), pltpu.VMEM((TOP_K, tm, d), F32),
                        pltpu.SemaphoreType.DMA, pltpu.SemaphoreType.DMA],
        compiler_params=_params("arbitrary"),
        name="moe_combine",
    )(pos.reshape(n // tm, tm * TOP_K), ys, h, probs, g.reshape(1, d))


def _group_tables(counts, top_i, rank, n_rows, tm):
    ne = counts.shape[0]
    padded = (counts + tm - 1) // tm * tm
    ends = jnp.cumsum(padded)
    offs = ends - padded
    pos = offs[top_i] + rank
    tile_start = jnp.arange(n_rows // tm, dtype=jnp.int32) * tm
    te = jnp.sum(tile_start[:, None] >= ends[None, :], axis=1).astype(jnp.int32)
    valid = tile_start < ends[-1]
    eidx = jnp.arange(ne, dtype=jnp.int32)
    nonempty = counts > 0
    first_e = jnp.min(jnp.where(nonempty, eidx, ne))
    last_e = jnp.max(jnp.where(nonempty, eidx, -1))
    later = jnp.logical_and(nonempty[None, :], eidx[None, :] > eidx[:, None])
    nxt_e = jnp.min(jnp.where(later, eidx[None, :], ne), axis=1)
    nxt_e = jnp.where(nxt_e == ne, first_e, nxt_e).astype(jnp.int32)
    te = jnp.where(valid, te, last_e).astype(jnp.int32)
    prev = jnp.concatenate([jnp.full((1,), -1, jnp.int32), te[:-1]])
    first = jnp.logical_and(valid, te != prev)
    meta = (te, valid.astype(jnp.int32), first.astype(jnp.int32), nxt_e[te],
            (te == last_e).astype(jnp.int32))
    return pos, meta


def moe_layer(h, xo, norm_g, router_w, router_b, w_gu, b_gu, w_dn, b_dn, final_g):
    n, d = h.shape
    ne = router_w.shape[1]
    h2, hn, top_i, probs, rank, counts = route(h, xo, norm_g, router_w, router_b)
    tm = 256 if n * TOP_K >= 256 * ne else 8
    n_rows = -(-(n * TOP_K) // tm) * tm + ne * tm
    pos, meta = _group_tables(counts[0], top_i, rank, n_rows, tm)
    xs = dispatch(hn, pos, n_rows)
    act = gmm(xs, w_gu, meta, tm, _tile(w_gu.shape[2], 1024), bias=b_gu, swiglu=True, packed=True, out_dtype=BF16)
    ys = gmm(act, w_dn, meta, tm, _tile(d, 1024), bias=b_dn)
    return combine(ys, pos, probs, h2, final_g)


def kernel(x_prompt, x_sample, mem_prompt, state_conv, state_lru, state_ret, cache_mem_k, cache_mem_v, norm_mix_g, w_in, conv_w, conv_b, lru_wa, lru_ba, lru_wx, lru_bx, lru_lambda, ret_norm_g, w_lru_branch, w_ret_branch, w_mix_out, norm_xa_g, norm_mem_g, xa_wq, xa_wk, xa_wv, xa_wo, norm_ffn_g, router_w, router_b, moe_w_gu, moe_b_gu, moe_w_dn, moe_b_dn, norm_final_g):
    depth = w_in.shape[0]
    assert depth == 1, "single-layer trunk"
    bp, tp, d = x_prompt.shape
    bs, ts, _ = x_sample.shape
    np_, ns = bp * tp, bs * ts
    n = np_ + ns
    n_mem = mem_prompt.shape[1]
    _, _, nh, dk, dv = state_ret.shape
    xh = cache_mem_k.shape[3]
    c_lru = conv_w.shape[-1]
    l = 0
    sizes = (c_lru, c_lru, nh * dk, nh * dk, nh * dv, nh * dv, d, d)
    cols = [0]
    for s in sizes:
        cols.append(cols[-1] + s)
    c_u, c_gl, c_q, c_k, c_v, c_gr, c_g1, c_g2 = cols[:8]
    assert c_u == 0 and c_gl == c_lru

    x = jnp.concatenate([x_prompt.reshape(np_, d), x_sample.reshape(ns, d)], axis=0)
    xn = rmsnorm(x, norm_mix_g[l], BF16)
    proj = dense(xn, w_in)

    lru_args = (conv_w[l], conv_b[l], lru_wa[l], lru_ba[l], lru_wx[l], lru_bx[l], lru_lambda[l])
    y_lru = jnp.zeros((n, c_lru), BF16)
    y_lru, pconv, plru = lru_branch(proj, y_lru, 0, bp, tp, jnp.zeros((bp, CONV_W - 1, c_lru), F32),
                                    jnp.zeros((bp, c_lru), F32), *lru_args)
    y_lru, sconv, slru = lru_branch(proj, y_lru, np_, bs, ts, state_conv[l], state_lru[l], *lru_args)
    y_ret = jnp.zeros((n, nh * dv), BF16)
    y_ret, pret = retention_branch(proj, y_ret, 0, bp, tp, 0, jnp.zeros((bp, nh, dk, dv), F32), ret_norm_g[l],
                                   c_q, c_k, c_v, c_gr)
    y_ret, sret = retention_branch(proj, y_ret, np_, bs, ts, PAST_LEN, state_ret[l], ret_norm_g[l],
                                   c_q, c_k, c_v, c_gr)
    m1 = dense(y_lru, w_lru_branch)
    m2 = dense(y_ret, w_ret_branch)
    merged = merge(m1, m2, proj, c_g1, c_g2)
    mix = dense(merged, w_mix_out)
    h, hn = add_rmsnorm(x, mix, norm_xa_g[l])

    mn = rmsnorm(mem_prompt.reshape(bp * n_mem, d), norm_mem_g[l], BF16)
    mk_p = dense(mn, xa_wk)
    mv_p = dense(mn, xa_wv)
    q = dense(hn, xa_wq, BF16)
    o = jnp.zeros((n, d), BF16)
    o = cross_attention(q, o, 0, bp, tp, mk_p, mv_p, xh)
    o = cross_attention(q, o, np_, bs, ts, cache_mem_k[l].reshape(bs * n_mem, d),
                        cache_mem_v[l].reshape(bs * n_mem, d), xh)
    xo = dense(o, xa_wo)

    y = moe_layer(h, xo, norm_ffn_g[l], router_w[l], router_b[l], moe_w_gu[l], moe_b_gu[l], moe_w_dn[l],
                  moe_b_dn[l], norm_final_g)

    hd = d // xh
    return (y[:np_].reshape(bp, tp, d), y[np_:].reshape(bs, ts, d),
            pconv[None], plru.reshape(1, bp, c_lru), pret[None],
            mk_p.reshape(1, bp, n_mem, xh, hd), mv_p.reshape(1, bp, n_mem, xh, hd),
            sconv[None], slru.reshape(1, bs, c_lru), sret[None])
```

```python
import functools
import math

import jax
import jax.numpy as jnp
from jax import lax
from jax.experimental import pallas as pl
from jax.experimental.pallas import tpu as pltpu

EPS = 1e-6
LRU_C = 8.0
ROPE_BASE = 10000.0
PAST_LEN = 16384
RET_CHUNK = 128
CONV_W = 4
TOP_K = 4
SWIGLU_LIMIT = 7.0
SWIGLU_ALPHA = 1.702

V7X_VMEM_LIMIT_BYTES = 56 * 1024 * 1024
V7X_MXU_COLS = 256
BF16 = jnp.bfloat16
F32 = jnp.float32


def _params(*sem):
    return pltpu.CompilerParams(dimension_semantics=sem, vmem_limit_bytes=V7X_VMEM_LIMIT_BYTES)


def _tile(n, pref):
    t = min(n, pref)
    while n % t:
        t //= 2
    return t


def _rms(x, g):
    return x * lax.rsqrt(jnp.mean(x * x, axis=-1, keepdims=True) + EPS) * g


def _rmsnorm_kernel(x_ref, g_ref, o_ref):
    o_ref[...] = _rms(x_ref[...], g_ref[...]).astype(o_ref.dtype)


def rmsnorm(x, g, out_dtype):
    n, d = x.shape
    tm = _tile(n, 512)
    return pl.pallas_call(
        _rmsnorm_kernel,
        grid=(n // tm,),
        in_specs=[pl.BlockSpec((tm, d), lambda i: (i, 0)), pl.BlockSpec((1, d), lambda i: (0, 0))],
        out_specs=pl.BlockSpec((tm, d), lambda i: (i, 0)),
        out_shape=jax.ShapeDtypeStruct((n, d), out_dtype),
        compiler_params=_params("parallel"),
        name="rmsnorm",
    )(x, g.reshape(1, d))


def _two_group_specs(na_rows, nb_rows, tm, d):
    na, nb = na_rows // tm, nb_rows // tm
    return (na, nb, pl.BlockSpec((tm, d), lambda i: (jnp.minimum(i, na - 1), 0)),
            pl.BlockSpec((tm, d), lambda i: (jnp.maximum(i - na, 0), 0)))


def _rmsnorm2_kernel(xa_ref, xb_ref, g_ref, o_ref, *, na):
    i = pl.program_id(0)

    @pl.when(i < na)
    def _():
        o_ref[...] = _rms(xa_ref[...], g_ref[...]).astype(o_ref.dtype)

    @pl.when(i >= na)
    def _():
        o_ref[...] = _rms(xb_ref[...], g_ref[...]).astype(o_ref.dtype)


def rmsnorm2(xa, xb, g, out_dtype):
    d = xa.shape[1]
    tm = math.gcd(_tile(xa.shape[0], 512), _tile(xb.shape[0], 512))
    na, nb, spec_a, spec_b = _two_group_specs(xa.shape[0], xb.shape[0], tm, d)
    return pl.pallas_call(
        functools.partial(_rmsnorm2_kernel, na=na),
        grid=(na + nb,),
        in_specs=[spec_a, spec_b, pl.BlockSpec((1, d), lambda i: (0, 0))],
        out_specs=pl.BlockSpec((tm, d), lambda i: (i, 0)),
        out_shape=jax.ShapeDtypeStruct((xa.shape[0] + xb.shape[0], d), out_dtype),
        compiler_params=_params("arbitrary"),
        name="rmsnorm2",
    )(xa, xb, g.reshape(1, d))


def _add_rmsnorm2_kernel(xa_ref, xb_ref, y_ref, g_ref, h_ref, hn_ref, *, na):
    i = pl.program_id(0)

    def emit(x_ref):
        h = x_ref[...] + y_ref[...]
        h_ref[...] = h
        hn_ref[...] = _rms(h, g_ref[...]).astype(hn_ref.dtype)

    @pl.when(i < na)
    def _():
        emit(xa_ref)

    @pl.when(i >= na)
    def _():
        emit(xb_ref)


def add_rmsnorm2(xa, xb, y, g):
    d = xa.shape[1]
    tm = math.gcd(_tile(xa.shape[0], 256), _tile(xb.shape[0], 256))
    na, nb, spec_a, spec_b = _two_group_specs(xa.shape[0], xb.shape[0], tm, d)
    n = xa.shape[0] + xb.shape[0]
    row = pl.BlockSpec((tm, d), lambda i: (i, 0))
    return pl.pallas_call(
        functools.partial(_add_rmsnorm2_kernel, na=na),
        grid=(na + nb,),
        in_specs=[spec_a, spec_b, row, pl.BlockSpec((1, d), lambda i: (0, 0))],
        out_specs=[row, row],
        out_shape=[jax.ShapeDtypeStruct((n, d), F32), jax.ShapeDtypeStruct((n, d), BF16)],
        compiler_params=_params("arbitrary"),
        name="add_rmsnorm2",
    )(xa, xb, y, g.reshape(1, d))


def _swiglu_pairs(h, sel):
    gate = jnp.minimum(h, SWIGLU_LIMIT)
    glu = gate * jax.nn.sigmoid(SWIGLU_ALPHA * gate)
    up1 = jnp.clip(h, -SWIGLU_LIMIT, SWIGLU_LIMIT) + 1.0
    prod = (glu * pltpu.roll(up1, h.shape[1] - 1, axis=1)).astype(BF16)
    return jnp.dot(prod, sel, preferred_element_type=F32)


def _pack_bf16_pair(hi, lo):
    hb = lax.bitcast_convert_type(hi.astype(BF16).astype(F32), jnp.uint32)
    lb = lax.bitcast_convert_type(lo.astype(BF16).astype(F32), jnp.uint32)
    return hb | (lb >> 16)


def _unpack_bf16_pair(word):
    return (lax.bitcast_convert_type(word & jnp.uint32(0xFFFF0000), F32),
            lax.bitcast_convert_type(word << 16, F32))


def _gmm_kernel(te_ref, tv_ref, first_ref, nxt_ref, lastg_ref, a_ref, w_hbm, *rest,
                tn, nj, cw, packed, has_bias, swiglu, gated_sum, pack_out):
    rest = list(rest)
    b_ref = rest.pop(0) if has_bias else None
    sel_ref = rest.pop(0) if swiglu else None
    if gated_sum:
        m1_ref, g1_ref, g2_ref = rest.pop(0), rest.pop(0), rest.pop(0)
    o_ref, wf_s, wb_s, sem = rest
    j = pl.program_id(0)
    t = pl.program_id(1)

    def w_copy(e, jj):
        return pltpu.make_async_copy(w_hbm.at[e, :, pl.ds(pl.multiple_of(jj * tn, tn), tn)], wf_s, sem)

    @pl.when(first_ref[t] == 1)
    def _():
        @pl.when(jnp.logical_and(j == 0, t == 0))
        def _():
            w_copy(te_ref[0], 0).start()

        w_copy(te_ref[t], j).wait()
        wb_s[...] = wf_s[...].astype(BF16)
        lastg = lastg_ref[t] == 1

        @pl.when(jnp.logical_not(jnp.logical_and(lastg, j == nj - 1)))
        def _():
            w_copy(nxt_ref[t], jnp.where(lastg, j + 1, j)).start()

    @pl.when(tv_ref[t] == 1)
    def _():
        if packed:
            half = a_ref.shape[1]
            x_hi, x_lo = (v.astype(BF16) for v in _unpack_bf16_pair(a_ref[...]))
        else:
            x = a_ref[...]
        ow = cw // 2 if (swiglu or pack_out) else cw
        for c in range(tn // cw):
            cols = pl.ds(c * cw, cw)
            if packed:
                h = (jnp.dot(x_hi, wb_s[pl.ds(0, half), cols], preferred_element_type=F32)
                     + jnp.dot(x_lo, wb_s[pl.ds(half, half), cols], preferred_element_type=F32))
            else:
                h = jnp.dot(x, wb_s[:, cols], preferred_element_type=F32)
            if has_bias:
                h = h + b_ref[:, cols]
            if swiglu:
                h = _swiglu_pairs(h, sel_ref[...])
            if gated_sum:
                h = jax.nn.sigmoid(g1_ref[:, cols]) * m1_ref[:, cols] + jax.nn.sigmoid(g2_ref[:, cols]) * h
            if pack_out:
                o_ref[:, pl.ds(c * ow, ow)] = _pack_bf16_pair(h[:, :ow], h[:, ow:])
            else:
                o_ref[:, pl.ds(c * ow, ow)] = h.astype(o_ref.dtype)

    @pl.when(tv_ref[t] == 0)
    def _():
        o_ref[...] = jnp.zeros_like(o_ref)


def _dense_meta(n_tiles):
    z = jnp.zeros((n_tiles,), jnp.int32)
    one = jnp.ones((n_tiles,), jnp.int32)
    return z, one, z.at[0].set(1), z, one


def gmm(a, w, meta, tm, tn, *, bias=None, swiglu=False, packed=False, gated_sum=None, pack_out=False,
        out_dtype=F32):
    r = a.shape[0]
    g, k, n = w.shape
    nj = n // tn
    cw = min(tn, 2 * V7X_MXU_COLS)
    halved = swiglu or pack_out
    ow_total = n // 2 if halved else n
    otn = tn // 2 if halved else tn
    in_specs = [
        pl.BlockSpec((tm, a.shape[1]), lambda j, t, *_: (t, 0)),
        pl.BlockSpec(memory_space=pl.ANY),
    ]
    args = [a, w]
    if bias is not None:
        in_specs.append(pl.BlockSpec((None, 1, tn), lambda j, t, te, *_: (te[t], 0, j)))
        args.append(bias.reshape(g, 1, n))
    if swiglu:
        in_specs.append(pl.BlockSpec((cw, cw // 2), lambda j, t, *_: (0, 0)))
        args.append((jnp.arange(cw)[:, None] == 2 * jnp.arange(cw // 2)[None, :]).astype(BF16))
    if gated_sum is not None:
        m1, gates, col1, col2 = gated_sum
        in_specs += [
            pl.BlockSpec((tm, tn), lambda j, t, *_: (t, j)),
            pl.BlockSpec((tm, tn), lambda j, t, *_: (t, col1 // tn + j)),
            pl.BlockSpec((tm, tn), lambda j, t, *_: (t, col2 // tn + j)),
        ]
        args += [m1, gates, gates]
    kern = functools.partial(_gmm_kernel, tn=tn, nj=nj, cw=cw, packed=packed, has_bias=bias is not None,
                             swiglu=swiglu, gated_sum=gated_sum is not None, pack_out=pack_out)
    grid_spec = pltpu.PrefetchScalarGridSpec(
        num_scalar_prefetch=5,
        grid=(nj, r // tm),
        in_specs=in_specs,
        out_specs=pl.BlockSpec((tm, otn), lambda j, t, *_: (t, j)),
        scratch_shapes=[pltpu.VMEM((k, tn), F32), pltpu.VMEM((k, tn), BF16), pltpu.SemaphoreType.DMA],
    )
    return pl.pallas_call(
        kern,
        grid_spec=grid_spec,
        out_shape=jax.ShapeDtypeStruct((r, ow_total), jnp.uint32 if pack_out else out_dtype),
        compiler_params=_params("arbitrary", "arbitrary"),
        name="gmm_swiglu" if swiglu else "gmm",
    )(*meta, *args)


def dense(a, w, out_dtype=F32, **kw):
    m, k = a.shape
    pref = 1024 if k <= 4096 else 512
    tm, tn = _tile(m, pref), _tile(w.shape[2], pref)
    return gmm(a, w, _dense_meta(m // tm), tm, tn, out_dtype=out_dtype, **kw)


def _gelu_tanh(x):
    c = math.sqrt(2.0 / math.pi)
    return x * (0.5 * (1.0 + jnp.tanh(c * (x + 0.044715 * (x * x * x)))))


def _softplus(z):
    return jnp.maximum(z, 0.0) + jnp.log1p(jnp.exp(-jnp.abs(z)))


def _lru_kernel(u_ref, g_ref, buf_ref, h0_ref, cw_ref, cb_ref, wa_ref, ba_ref, wx_ref, bx_ref, lam_ref, yin_ref,
                y_ref, nbuf_ref, ht_ref, ext_s, a_s, b_s, hs_s, h_s, *, tt, nblk, bw):
    del yin_ref
    t = pl.program_id(1)
    nt = pl.num_programs(1)
    halo = CONV_W - 1

    @pl.when(t == 0)
    def _():
        ext_s[pl.ds(8 - halo, halo), :] = buf_ref[...]
        h_s[...] = h0_ref[...]

    ext_s[pl.ds(8, tt), :] = u_ref[...]
    uc = cb_ref[...] + sum(ext_s[pl.ds(8 - halo + j, tt), :] * cw_ref[pl.ds(j, 1), :] for j in range(CONV_W))
    tail = ext_s[pl.ds(8 + tt - halo, halo), :]
    ext_s[pl.ds(8 - halo, halo), :] = tail

    sp = _softplus(-lam_ref[...])
    for n in range(nblk):
        cs = slice(n * bw, (n + 1) * bw)
        ub = uc[:, cs]
        ubb = ub.astype(BF16)
        r = jax.nn.sigmoid(jnp.dot(ubb, wa_ref[n].astype(BF16), preferred_element_type=F32) + ba_ref[:, cs])
        i = jax.nn.sigmoid(jnp.dot(ubb, wx_ref[n].astype(BF16), preferred_element_type=F32) + bx_ref[:, cs])
        log_a = -LRU_C * r * sp[:, cs]
        a_s[:, cs] = jnp.exp(log_a)
        one_minus_a2 = -jnp.tanh(log_a) * (jnp.exp(2.0 * log_a) + 1.0)
        b_s[:, cs] = jnp.sqrt(one_minus_a2) * (i * ub)

    def step(s, h):
        h = a_s[pl.ds(s, 1), :] * h + b_s[pl.ds(s, 1), :]
        hs_s[pl.ds(s, 1), :] = h
        return h

    h = lax.fori_loop(0, tt, step, h_s[...], unroll=8)
    h_s[...] = h
    y_ref[...] = (_gelu_tanh(g_ref[...]) * hs_s[...]).astype(y_ref.dtype)

    @pl.when(t == nt - 1)
    def _():
        nbuf_ref[...] = tail
        ht_ref[...] = h


def lru_branch(proj, y_all, row0, nb, t_len, conv_buf, h0, conv_w, conv_b, wa, ba, wx, bx, lam):
    c = conv_w.shape[-1]
    nblk, bw, _ = wa.shape
    tt = _tile(t_len, 256)
    nt = t_len // tt
    rb0 = row0 // tt
    vec = pl.BlockSpec((1, c), lambda b, t: (0, 0))
    kern = functools.partial(_lru_kernel, tt=tt, nblk=nblk, bw=bw)
    return pl.pallas_call(
        kern,
        grid=(nb, nt),
        in_specs=[
            pl.BlockSpec((tt, c), lambda b, t: (rb0 + b * nt + t, 0)),
            pl.BlockSpec((tt, c), lambda b, t: (rb0 + b * nt + t, 1)),
            pl.BlockSpec((None, CONV_W - 1, c), lambda b, t: (b, 0, 0)),
            pl.BlockSpec((None, 1, c), lambda b, t: (b, 0, 0)),
            pl.BlockSpec((CONV_W, c), lambda b, t: (0, 0)),
            vec,
            pl.BlockSpec((nblk, bw, bw), lambda b, t: (0, 0, 0)),
            vec,
            pl.BlockSpec((nblk, bw, bw), lambda b, t: (0, 0, 0)),
            vec,
            vec,
            pl.BlockSpec(memory_space=pl.ANY),
        ],
        out_specs=[
            pl.BlockSpec((tt, c), lambda b, t: (rb0 + b * nt + t, 0)),
            pl.BlockSpec((None, CONV_W - 1, c), lambda b, t: (b, 0, 0)),
            pl.BlockSpec((None, 1, c), lambda b, t: (b, 0, 0)),
        ],
        out_shape=[
            jax.ShapeDtypeStruct(y_all.shape, y_all.dtype),
            jax.ShapeDtypeStruct((nb, CONV_W - 1, c), F32),
            jax.ShapeDtypeStruct((nb, 1, c), F32),
        ],
        scratch_shapes=[
            pltpu.VMEM((tt + 8, c), F32),
            pltpu.VMEM((tt, c), F32),
            pltpu.VMEM((tt, c), F32),
            pltpu.VMEM((tt, c), F32),
            pltpu.VMEM((1, c), F32),
        ],
        input_output_aliases={11: 0},
        compiler_params=_params("parallel", "arbitrary"),
        name="lru_branch",
    )(proj, proj, conv_buf, h0.reshape(nb, 1, c), conv_w, conv_b.reshape(1, c), wa, ba.reshape(1, c), wx,
      bx.reshape(1, c), lam.reshape(1, c), y_all)


def _rotate(x, cos, sin):
    half = x.shape[-1] // 2
    x1, x2 = x[:, :half], x[:, half:]
    return jnp.concatenate([x1 * cos - x2 * sin, x1 * sin + x2 * cos], axis=-1)


def _ret_kernel(q_ref, k_ref, v_ref, g_ref, cos_ref, sin_ref, dec_ref, xi_ref, zeta_ref, gch_ref, s0_ref, ng_ref,
                yin_ref, y_ref, s_ref, *, dk, dv, hb):
    del yin_ref
    c = pl.program_id(2)

    @pl.when(c == 0)
    def _():
        s_ref[...] = s0_ref[...]

    cos, sin = cos_ref[...], sin_ref[...]
    for hh in range(hb):
        kc, vc = pl.ds(hh * dk, dk), pl.ds(hh * dv, dv)
        q = _rotate(q_ref[:, kc], cos, sin)
        k = _rotate(k_ref[:, kc], cos, sin) * (dk ** -0.5)
        vb = v_ref[:, vc].astype(BF16)
        qb = q.astype(BF16)
        s_old = s_ref[hh]
        scores = lax.dot_general(qb, k.astype(BF16), (((1,), (1,)), ((), ())),
                                 preferred_element_type=F32) * dec_ref[hh]
        o = jnp.dot(scores.astype(BF16), vb, preferred_element_type=F32)
        o = o + jnp.dot(qb, s_old.astype(BF16), preferred_element_type=F32) * xi_ref[hh]
        kz = (k * zeta_ref[hh]).astype(BF16)
        s_ref[hh] = gch_ref[hh] * s_old + lax.dot_general(kz, vb, (((0,), (0,)), ((), ())),
                                                          preferred_element_type=F32)
        o = o * lax.rsqrt(jnp.mean(o * o, axis=-1, keepdims=True) + EPS) * ng_ref[hh]
        g = g_ref[:, vc]
        y_ref[:, vc] = (g * jax.nn.sigmoid(g) * o).astype(y_ref.dtype)


def retention_branch(proj, y_all, row0, nb, t_len, pos0, s0, ret_norm_g, col_q, col_k, col_v, col_g):
    _, nh, dk, dv = s0.shape
    ch = RET_CHUNK if t_len % RET_CHUNK == 0 else t_len
    nc = t_len // ch
    rb0 = row0 // ch
    half = dk // 2
    hb = _tile(nh, 4)
    pos = (pos0 + jnp.arange(t_len, dtype=jnp.int32)).astype(F32)
    inv = ROPE_BASE ** (-jnp.arange(half, dtype=F32) / half)
    ang = pos[:, None] * inv[None, :]
    cos, sin = jnp.cos(ang), jnp.sin(ang)
    log_g = jnp.log1p(-jnp.exp2(-5.0 - jnp.arange(nh, dtype=F32)))
    idx = jnp.arange(ch, dtype=F32)
    diff = idx[:, None] - idx[None, :]
    dec = jnp.where(diff[None] >= 0, jnp.exp(jnp.maximum(diff, 0.0)[None] * log_g[:, None, None]), 0.0)
    xi = jnp.exp((idx + 1.0)[None, :, None] * log_g[:, None, None])
    zeta = jnp.exp((ch - 1.0 - idx)[None, :, None] * log_g[:, None, None])
    gch = jnp.exp(ch * log_g)[:, None, None]

    qb, kb, vb, gb = col_q // (hb * dk), col_k // (hb * dk), col_v // (hb * dv), col_g // (hb * dv)
    row = lambda b, h, c: rb0 + b * nc + c
    kern = functools.partial(_ret_kernel, dk=dk, dv=dv, hb=hb)
    return pl.pallas_call(
        kern,
        grid=(nb, nh // hb, nc),
        in_specs=[
            pl.BlockSpec((ch, hb * dk), lambda b, h, c: (row(b, h, c), qb + h)),
            pl.BlockSpec((ch, hb * dk), lambda b, h, c: (row(b, h, c), kb + h)),
            pl.BlockSpec((ch, hb * dv), lambda b, h, c: (row(b, h, c), vb + h)),
            pl.BlockSpec((ch, hb * dv), lambda b, h, c: (row(b, h, c), gb + h)),
            pl.BlockSpec((ch, half), lambda b, h, c: (c, 0)),
            pl.BlockSpec((ch, half), lambda b, h, c: (c, 0)),
            pl.BlockSpec((hb, ch, ch), lambda b, h, c: (h, 0, 0)),
            pl.BlockSpec((hb, ch, 1), lambda b, h, c: (h, 0, 0)),
            pl.BlockSpec((hb, ch, 1), lambda b, h, c: (h, 0, 0)),
            pl.BlockSpec((hb, 1, 1), lambda b, h, c: (h, 0, 0)),
            pl.BlockSpec((None, hb, dk, dv), lambda b, h, c: (b, h, 0, 0)),
            pl.BlockSpec((hb, 1, dv), lambda b, h, c: (h, 0, 0)),
            pl.BlockSpec(memory_space=pl.ANY),
        ],
        out_specs=[
            pl.BlockSpec((ch, hb * dv), lambda b, h, c: (row(b, h, c), h)),
            pl.BlockSpec((None, hb, dk, dv), lambda b, h, c: (b, h, 0, 0)),
        ],
        out_shape=[
            jax.ShapeDtypeStruct(y_all.shape, y_all.dtype),
            jax.ShapeDtypeStruct((nb, nh, dk, dv), F32),
        ],
        input_output_aliases={12: 0},
        compiler_params=_params("parallel", "parallel", "arbitrary"),
        name="retention",
    )(proj, proj, proj, proj, cos, sin, dec, xi, zeta, gch, s0, ret_norm_g.reshape(nh, 1, dv), y_all)


def _xattn_kernel(q_ref, k_ref, v_ref, oin_ref, o_ref, *, scale, nh, hd):
    del oin_ref
    for h in range(nh):
        cols = pl.ds(h * hd, hd)
        kb = k_ref[:, cols].astype(BF16)
        vb = v_ref[:, cols].astype(BF16)
        s = lax.dot_general(q_ref[:, cols], kb, (((1,), (1,)), ((), ())), preferred_element_type=F32) * scale
        e = jnp.exp(s - jnp.max(s, axis=-1, keepdims=True))
        p = e / jnp.sum(e, axis=-1, keepdims=True)
        o_ref[:, cols] = jnp.dot(p.astype(BF16), vb, preferred_element_type=F32).astype(o_ref.dtype)


def cross_attention(q, o_all, row0, nb, t_len, mk, mv, nh):
    d = q.shape[1]
    hd = d // nh
    n_mem = mk.shape[0] // nb
    tq = _tile(t_len, 512)
    nt = t_len // tq
    rb0 = row0 // tq
    kern = functools.partial(_xattn_kernel, scale=hd ** -0.5, nh=nh, hd=hd)
    return pl.pallas_call(
        kern,
        grid=(nb, nt),
        in_specs=[
            pl.BlockSpec((tq, d), lambda b, t: (rb0 + b * nt + t, 0)),
            pl.BlockSpec((n_mem, d), lambda b, t: (b, 0)),
            pl.BlockSpec((n_mem, d), lambda b, t: (b, 0)),
            pl.BlockSpec(memory_space=pl.ANY),
        ],
        out_specs=pl.BlockSpec((tq, d), lambda b, t: (rb0 + b * nt + t, 0)),
        out_shape=jax.ShapeDtypeStruct(o_all.shape, o_all.dtype),
        input_output_aliases={3: 0},
        compiler_params=_params("parallel", "parallel"),
        name="cross_attention",
    )(q, mk, mv, o_all)


def _route_kernel(h_ref, xo_ref, g_ref, rw_ref, rb_ref, h2_ref, hn_ref, ti_ref, pr_ref, rk_ref, cnt_ref, cnt_s,
                  *, tm, ne):
    i = pl.program_id(0)

    @pl.when(i == 0)
    def _():
        cnt_s[...] = jnp.zeros_like(cnt_s)

    h2 = h_ref[...] + xo_ref[...]
    h2_ref[...] = h2
    xn = _rms(h2, g_ref[...])
    half = xn.shape[1] // 2
    hn_ref[...] = _pack_bf16_pair(xn[:, :half], xn[:, half:])
    logits = jnp.dot(xn, rw_ref[...], preferred_element_type=F32, precision=lax.Precision.HIGHEST) + rb_ref[...]
    lane = lax.broadcasted_iota(jnp.int32, (tm, ne), 1)
    vals, idxs, hots = [], [], []
    for _ in range(TOP_K):
        m = jnp.max(logits, axis=-1, keepdims=True)
        idx = jnp.min(jnp.where(logits == m, lane, ne), axis=-1, keepdims=True)
        hot = lane == idx
        vals.append(m)
        idxs.append(idx)
        hots.append(hot.astype(F32))
        logits = jnp.where(hot, -jnp.inf, logits)
    es = [jnp.exp(v - vals[0]) for v in vals]
    den = es[0] + es[1] + es[2] + es[3]
    pr_ref[...] = jnp.concatenate([e / den for e in es], axis=-1)
    ti_ref[...] = jnp.concatenate(idxs, axis=-1)
    oh = hots[0] + hots[1] + hots[2] + hots[3]
    r = lax.broadcasted_iota(jnp.int32, (tm, tm), 0)
    cidx = lax.broadcasted_iota(jnp.int32, (tm, tm), 1)
    tri = (r > cidx).astype(BF16)
    base = cnt_s[...] + jnp.dot(tri, oh.astype(BF16), preferred_element_type=F32)
    rk_ref[...] = jnp.concatenate([jnp.sum(hk * base, axis=-1, keepdims=True) for hk in hots],
                                  axis=-1).astype(jnp.int32)
    cnt = cnt_s[...] + jnp.sum(oh, axis=0, keepdims=True)
    cnt_s[...] = cnt
    cnt_ref[...] = cnt.astype(jnp.int32)


def route(h, xo, g, router_w, router_b):
    n, d = h.shape
    ne = router_w.shape[1]
    tm = _tile(n, 256)
    row = pl.BlockSpec((tm, d), lambda i: (i, 0))
    small = pl.BlockSpec((tm, TOP_K), lambda i: (i, 0))
    kern = functools.partial(_route_kernel, tm=tm, ne=ne)
    return pl.pallas_call(
        kern,
        grid=(n // tm,),
        in_specs=[row, row, pl.BlockSpec((1, d), lambda i: (0, 0)), pl.BlockSpec((d, ne), lambda i: (0, 0)),
                  pl.BlockSpec((1, ne), lambda i: (0, 0))],
        out_specs=[row, pl.BlockSpec((tm, d // 2), lambda i: (i, 0)), small, small, small,
                   pl.BlockSpec((1, ne), lambda i: (0, 0))],
        out_shape=[
            jax.ShapeDtypeStruct((n, d), F32),
            jax.ShapeDtypeStruct((n, d // 2), jnp.uint32),
            jax.ShapeDtypeStruct((n, TOP_K), jnp.int32),
            jax.ShapeDtypeStruct((n, TOP_K), F32),
            jax.ShapeDtypeStruct((n, TOP_K), jnp.int32),
            jax.ShapeDtypeStruct((1, ne), jnp.int32),
        ],
        scratch_shapes=[pltpu.VMEM((1, ne), F32)],
        compiler_params=_params("arbitrary"),
        name="route",
    )(h, xo, g.reshape(1, d), router_w, router_b.reshape(1, ne))


def _dispatch_kernel(pos_hbm, x_ref, xs_in, xs_hbm, pos_s, sem_p, sem, *, tm):
    del xs_in
    i = pl.program_id(0)
    cp = pltpu.make_async_copy(pos_hbm.at[i], pos_s, sem_p)
    cp.start()
    cp.wait()

    def row_copy(r, kk):
        return pltpu.make_async_copy(x_ref.at[pl.ds(r, 1)], xs_hbm.at[pl.ds(pos_s[r * TOP_K + kk], 1)], sem)

    def issue(r, carry):
        for kk in range(TOP_K):
            row_copy(r, kk).start()
        return carry

    lax.fori_loop(0, tm, issue, 0)
    for _ in range(TOP_K):
        pltpu.make_async_copy(x_ref, xs_hbm.at[pl.ds(0, tm)], sem).wait()


def dispatch(x, pos, n_rows):
    n, d = x.shape
    tm = _tile(n, 256)
    kern = functools.partial(_dispatch_kernel, tm=tm)
    return pl.pallas_call(
        kern,
        grid=(n // tm,),
        in_specs=[pl.BlockSpec(memory_space=pl.ANY), pl.BlockSpec((tm, d), lambda i: (i, 0)),
                  pl.BlockSpec(memory_space=pl.ANY)],
        out_specs=pl.BlockSpec(memory_space=pl.ANY),
        out_shape=jax.ShapeDtypeStruct((n_rows, d), x.dtype),
        scratch_shapes=[pltpu.SMEM((tm * TOP_K,), jnp.int32), pltpu.SemaphoreType.DMA, pltpu.SemaphoreType.DMA],
        input_output_aliases={2: 0},
        compiler_params=_params("arbitrary"),
        name="moe_dispatch",
    )(pos.reshape(n // tm, tm * TOP_K), x, jnp.zeros((n_rows, d), x.dtype))


def _combine_kernel(pos_hbm, ys_hbm, h_ref, p_ref, g_ref, o_ref, pos_s, buf, sem_p, sem, *, tm, pw):
    i = pl.program_id(0)
    nt = pl.num_programs(0)
    slot = i % 2

    def fetch(tile, sl):
        cp = pltpu.make_async_copy(pos_hbm.at[tile], pos_s.at[sl], sem_p)
        cp.start()
        cp.wait()

        def issue(r, carry):
            for kk in range(TOP_K):
                pltpu.make_async_copy(ys_hbm.at[pl.ds(pos_s[sl, r * TOP_K + kk], 1)],
                                      buf.at[sl, pl.ds(kk * tm + r, 1)], sem.at[sl]).start()
            return carry

        lax.fori_loop(0, tm, issue, 0)

    @pl.when(i == 0)
    def _():
        fetch(0, 0)

    @pl.when(i + 1 < nt)
    def _():
        fetch(i + 1, 1 - slot)

    pltpu.make_async_copy(ys_hbm.at[pl.ds(0, TOP_K * tm)], buf.at[slot], sem.at[slot]).wait()
    p = p_ref[...]
    parts = []
    for c in range(buf.shape[2] // pw):
        acc_hi = acc_lo = None
        for kk in range(TOP_K):
            hi, lo = _unpack_bf16_pair(buf[slot, pl.ds(kk * tm, tm), pl.ds(c * pw, pw)])
            w = p[:, kk:kk + 1]
            acc_hi = w * hi if acc_hi is None else acc_hi + w * hi
            acc_lo = w * lo if acc_lo is None else acc_lo + w * lo
        parts += [acc_hi, acc_lo]
    o_ref[...] = _rms(h_ref[...] + jnp.concatenate(parts, axis=-1), g_ref[...])


def combine(ys, pos, probs, h, g, pw):
    n, d = h.shape
    tm = _tile(n, 128)
    kern = functools.partial(_combine_kernel, tm=tm, pw=pw)
    row = pl.BlockSpec((tm, d), lambda i: (i, 0))
    return pl.pallas_call(
        kern,
        grid=(n // tm,),
        in_specs=[pl.BlockSpec(memory_space=pl.ANY), pl.BlockSpec(memory_space=pl.ANY), row,
                  pl.BlockSpec((tm, TOP_K), lambda i: (i, 0)), pl.BlockSpec((1, d), lambda i: (0, 0))],
        out_specs=row,
        out_shape=jax.ShapeDtypeStruct((n, d), F32),
        scratch_shapes=[pltpu.SMEM((2, tm * TOP_K), jnp.int32), pltpu.VMEM((2, TOP_K * tm, d // 2), jnp.uint32),
                        pltpu.SemaphoreType.DMA, pltpu.SemaphoreType.DMA((2,))],
        compiler_params=_params("arbitrary"),
        name="moe_combine",
    )(pos.reshape(n // tm, tm * TOP_K), ys, h, probs, g.reshape(1, d))


def _group_tables(counts, top_i, rank, n_rows, tm):
    ne = counts.shape[0]
    padded = (counts + tm - 1) // tm * tm
    ends = jnp.cumsum(padded)
    offs = ends - padded
    pos = offs[top_i] + rank
    tile_start = jnp.arange(n_rows // tm, dtype=jnp.int32) * tm
    te = jnp.sum(tile_start[:, None] >= ends[None, :], axis=1).astype(jnp.int32)
    valid = tile_start < ends[-1]
    eidx = jnp.arange(ne, dtype=jnp.int32)
    nonempty = counts > 0
    first_e = jnp.min(jnp.where(nonempty, eidx, ne))
    last_e = jnp.max(jnp.where(nonempty, eidx, 0))
    later = jnp.logical_and(nonempty[None, :], eidx[None, :] > eidx[:, None])
    nxt_e = jnp.min(jnp.where(later, eidx[None, :], ne), axis=1)
    nxt_e = jnp.where(nxt_e == ne, first_e, nxt_e).astype(jnp.int32)
    te = jnp.where(valid, te, last_e).astype(jnp.int32)
    prev = jnp.concatenate([jnp.full((1,), -1, jnp.int32), te[:-1]])
    first = jnp.logical_and(valid, te != prev)
    meta = (te, valid.astype(jnp.int32), first.astype(jnp.int32), nxt_e[te],
            (te == last_e).astype(jnp.int32))
    return pos, meta


def moe_layer(h, xo, norm_g, router_w, router_b, w_gu, b_gu, w_dn, b_dn, final_g):
    n, d = h.shape
    ne = router_w.shape[1]
    h2, hn, top_i, probs, rank, counts = route(h, xo, norm_g, router_w, router_b)
    tm = 256 if n * TOP_K >= 256 * ne else 8
    n_rows = -(-(n * TOP_K) // tm) * tm + ne * tm
    pos, meta = _group_tables(counts[0], top_i, rank, n_rows, tm)
    xs = dispatch(hn, pos, n_rows)
    act = gmm(xs, w_gu, meta, tm, _tile(w_gu.shape[2], 1024), bias=b_gu, swiglu=True, packed=True, out_dtype=BF16)
    tn = _tile(d, 1024)
    ys = gmm(act, w_dn, meta, tm, tn, bias=b_dn, pack_out=True)
    return combine(ys, pos, probs, h2, final_g, min(tn, 2 * V7X_MXU_COLS) // 2)


def kernel(x_prompt, x_sample, mem_prompt, state_conv, state_lru, state_ret, cache_mem_k, cache_mem_v, norm_mix_g, w_in, conv_w, conv_b, lru_wa, lru_ba, lru_wx, lru_bx, lru_lambda, ret_norm_g, w_lru_branch, w_ret_branch, w_mix_out, norm_xa_g, norm_mem_g, xa_wq, xa_wk, xa_wv, xa_wo, norm_ffn_g, router_w, router_b, moe_w_gu, moe_b_gu, moe_w_dn, moe_b_dn, norm_final_g):
    depth = w_in.shape[0]
    assert depth == 1, "single-layer trunk"
    bp, tp, d = x_prompt.shape
    bs, ts, _ = x_sample.shape
    np_, ns = bp * tp, bs * ts
    n = np_ + ns
    n_mem = mem_prompt.shape[1]
    _, _, nh, dk, dv = state_ret.shape
    xh = cache_mem_k.shape[3]
    c_lru = conv_w.shape[-1]
    l = 0
    sizes = (c_lru, c_lru, nh * dk, nh * dk, nh * dv, nh * dv, d, d)
    cols = [0]
    for s in sizes:
        cols.append(cols[-1] + s)
    c_u, c_gl, c_q, c_k, c_v, c_gr, c_g1, c_g2 = cols[:8]
    assert c_u == 0 and c_gl == c_lru

    xp2, xs2 = x_prompt.reshape(np_, d), x_sample.reshape(ns, d)
    xn = rmsnorm2(xp2, xs2, norm_mix_g[l], BF16)
    proj = dense(xn, w_in)

    lru_args = (conv_w[l], conv_b[l], lru_wa[l], lru_ba[l], lru_wx[l], lru_bx[l], lru_lambda[l])
    y_lru = jnp.zeros((n, c_lru), BF16)
    y_lru, pconv, plru = lru_branch(proj, y_lru, 0, bp, tp, jnp.zeros((bp, CONV_W - 1, c_lru), F32),
                                    jnp.zeros((bp, c_lru), F32), *lru_args)
    y_lru, sconv, slru = lru_branch(proj, y_lru, np_, bs, ts, state_conv[l], state_lru[l], *lru_args)
    y_ret = jnp.zeros((n, nh * dv), BF16)
    y_ret, pret = retention_branch(proj, y_ret, 0, bp, tp, 0, jnp.zeros((bp, nh, dk, dv), F32), ret_norm_g[l],
                                   c_q, c_k, c_v, c_gr)
    y_ret, sret = retention_branch(proj, y_ret, np_, bs, ts, PAST_LEN, state_ret[l], ret_norm_g[l],
                                   c_q, c_k, c_v, c_gr)
    m1 = dense(y_lru, w_lru_branch)
    merged = dense(y_ret, w_ret_branch, BF16, gated_sum=(m1, proj, c_g1, c_g2))
    mix = dense(merged, w_mix_out)
    h, hn = add_rmsnorm2(xp2, xs2, mix, norm_xa_g[l])

    mn = rmsnorm(mem_prompt.reshape(bp * n_mem, d), norm_mem_g[l], BF16)
    mk_p = dense(mn, xa_wk)
    mv_p = dense(mn, xa_wv)
    q = dense(hn, xa_wq, BF16)
    o = jnp.zeros((n, d), BF16)
    o = cross_attention(q, o, 0, bp, tp, mk_p, mv_p, xh)
    o = cross_attention(q, o, np_, bs, ts, cache_mem_k[l].reshape(bs * n_mem, d),
                        cache_mem_v[l].reshape(bs * n_mem, d), xh)
    xo = dense(o, xa_wo)

    y = moe_layer(h, xo, norm_ffn_g[l], router_w[l], router_b[l], moe_w_gu[l], moe_b_gu[l], moe_w_dn[l],
                  moe_b_dn[l], norm_final_g)

    hd = d // xh
    return (y[:np_].reshape(bp, tp, d), y[np_:].reshape(bs, ts, d),
            pconv[None], plru.reshape(1, bp, c_lru), pret[None],
            mk_p.reshape(1, bp, n_mem, xh, hd), mv_p.reshape(1, bp, n_mem, xh, hd),
            sconv[None], slru.reshape(1, bs, c_lru), sret[None])
```

```python
import functools
import math

import jax
import jax.numpy as jnp
from jax import lax
from jax.experimental import pallas as pl
from jax.experimental.pallas import tpu as pltpu

EPS = 1e-6
LRU_C = 8.0
ROPE_BASE = 10000.0
PAST_LEN = 16384
RET_CHUNK = 128
CONV_W = 4
TOP_K = 4
SWIGLU_LIMIT = 7.0
SWIGLU_ALPHA = 1.702

V7X_VMEM_LIMIT_BYTES = 56 * 1024 * 1024
V7X_MXU_COLS = 256
WEIGHT_CHUNK_ROWS = 512
WEIGHT_CHUNKS_IN_FLIGHT = 3
BF16 = jnp.bfloat16
F32 = jnp.float32


def _params(*sem):
    return pltpu.CompilerParams(dimension_semantics=sem, vmem_limit_bytes=V7X_VMEM_LIMIT_BYTES)


def _tile(n, pref):
    t = min(n, pref)
    while n % t:
        t //= 2
    return t


def _rms(x, g):
    return x * lax.rsqrt(jnp.mean(x * x, axis=-1, keepdims=True) + EPS) * g


def _rmsnorm_kernel(x_ref, g_ref, o_ref):
    o_ref[...] = _rms(x_ref[...], g_ref[...]).astype(o_ref.dtype)


def rmsnorm(x, g, out_dtype):
    n, d = x.shape
    tm = _tile(n, 512)
    return pl.pallas_call(
        _rmsnorm_kernel,
        grid=(n // tm,),
        in_specs=[pl.BlockSpec((tm, d), lambda i: (i, 0)), pl.BlockSpec((1, d), lambda i: (0, 0))],
        out_specs=pl.BlockSpec((tm, d), lambda i: (i, 0)),
        out_shape=jax.ShapeDtypeStruct((n, d), out_dtype),
        compiler_params=_params("parallel"),
        name="rmsnorm",
    )(x, g.reshape(1, d))


def _two_group_specs(na_rows, nb_rows, tm, d):
    na, nb = na_rows // tm, nb_rows // tm
    return (na, nb, pl.BlockSpec((tm, d), lambda i: (jnp.minimum(i, na - 1), 0)),
            pl.BlockSpec((tm, d), lambda i: (jnp.maximum(i - na, 0), 0)))


def _rmsnorm2_kernel(xa_ref, xb_ref, g_ref, o_ref, *, na):
    i = pl.program_id(0)

    @pl.when(i < na)
    def _():
        o_ref[...] = _rms(xa_ref[...], g_ref[...]).astype(o_ref.dtype)

    @pl.when(i >= na)
    def _():
        o_ref[...] = _rms(xb_ref[...], g_ref[...]).astype(o_ref.dtype)


def rmsnorm2(xa, xb, g, out_dtype):
    d = xa.shape[1]
    tm = math.gcd(_tile(xa.shape[0], 512), _tile(xb.shape[0], 512))
    na, nb, spec_a, spec_b = _two_group_specs(xa.shape[0], xb.shape[0], tm, d)
    return pl.pallas_call(
        functools.partial(_rmsnorm2_kernel, na=na),
        grid=(na + nb,),
        in_specs=[spec_a, spec_b, pl.BlockSpec((1, d), lambda i: (0, 0))],
        out_specs=pl.BlockSpec((tm, d), lambda i: (i, 0)),
        out_shape=jax.ShapeDtypeStruct((xa.shape[0] + xb.shape[0], d), out_dtype),
        compiler_params=_params("arbitrary"),
        name="rmsnorm2",
    )(xa, xb, g.reshape(1, d))


def _add_rmsnorm2_kernel(xa_ref, xb_ref, y_ref, g_ref, h_ref, hn_ref, *, na):
    i = pl.program_id(0)

    def emit(x_ref):
        h = x_ref[...] + y_ref[...]
        h_ref[...] = h
        hn_ref[...] = _rms(h, g_ref[...]).astype(hn_ref.dtype)

    @pl.when(i < na)
    def _():
        emit(xa_ref)

    @pl.when(i >= na)
    def _():
        emit(xb_ref)


def add_rmsnorm2(xa, xb, y, g):
    d = xa.shape[1]
    tm = math.gcd(_tile(xa.shape[0], 256), _tile(xb.shape[0], 256))
    na, nb, spec_a, spec_b = _two_group_specs(xa.shape[0], xb.shape[0], tm, d)
    n = xa.shape[0] + xb.shape[0]
    row = pl.BlockSpec((tm, d), lambda i: (i, 0))
    return pl.pallas_call(
        functools.partial(_add_rmsnorm2_kernel, na=na),
        grid=(na + nb,),
        in_specs=[spec_a, spec_b, row, pl.BlockSpec((1, d), lambda i: (0, 0))],
        out_specs=[row, row],
        out_shape=[jax.ShapeDtypeStruct((n, d), F32), jax.ShapeDtypeStruct((n, d), BF16)],
        compiler_params=_params("arbitrary"),
        name="add_rmsnorm2",
    )(xa, xb, y, g.reshape(1, d))


def _swiglu_pairs(h, sel):
    gate = jnp.minimum(h, SWIGLU_LIMIT)
    glu = gate * jax.nn.sigmoid(SWIGLU_ALPHA * gate)
    up1 = jnp.clip(h, -SWIGLU_LIMIT, SWIGLU_LIMIT) + 1.0
    prod = (glu * pltpu.roll(up1, h.shape[1] - 1, axis=1)).astype(BF16)
    return jnp.dot(prod, sel, preferred_element_type=F32)


def _pack_bf16_pair(hi, lo):
    hb = lax.bitcast_convert_type(hi.astype(BF16).astype(F32), jnp.uint32)
    lb = lax.bitcast_convert_type(lo.astype(BF16).astype(F32), jnp.uint32)
    return hb | (lb >> 16)


def _unpack_bf16_pair(word):
    return (lax.bitcast_convert_type(word & jnp.uint32(0xFFFF0000), F32),
            lax.bitcast_convert_type(word << 16, F32))


def _gmm_kernel(te_ref, tv_ref, first_ref, nxt_ref, lastg_ref, quota_ref, a_ref, w_hbm, *rest,
                tn, nj, cw, kc, nc, ns, packed, has_bias, swiglu, gated_sum, pack_out):
    rest = list(rest)
    b_ref = rest.pop(0) if has_bias else None
    sel_ref = rest.pop(0) if swiglu else None
    if gated_sum:
        m1_ref, g1_ref, g2_ref = rest.pop(0), rest.pop(0), rest.pop(0)
    o_ref, ring_s, wb_s, st_s, sem = rest
    j = pl.program_id(0)
    t = pl.program_id(1)

    def chunk_copy(e, jj, c):
        slot = c % ns
        return pltpu.make_async_copy(
            w_hbm.at[e, pl.ds(pl.multiple_of(c * kc, kc), kc), pl.ds(pl.multiple_of(jj * tn, tn), tn)],
            ring_s.at[slot], sem.at[slot])

    def prime(e, jj):
        for c in range(ns):
            chunk_copy(e, jj, c).start()
        st_s[1] = ns
        st_s[2] = 0

    def convert_chunks(e, jj, buf, n):
        def body(_, carry):
            c = st_s[2]
            chunk_copy(e, jj, c).wait()
            wb_s[buf, pl.ds(pl.multiple_of(c * kc, kc), kc), :] = ring_s[c % ns].astype(BF16)
            st_s[2] = c + 1
            s = st_s[1]

            @pl.when(s < nc)
            def _():
                chunk_copy(e, jj, s).start()
                st_s[1] = s + 1

            return carry

        lax.fori_loop(0, n, body, 0)

    lastg = lastg_ref[t] == 1
    has_next = jnp.logical_not(jnp.logical_and(lastg, j == nj - 1))
    nxt_e = nxt_ref[t]
    nxt_j = jnp.where(lastg, j + 1, j)

    @pl.when(first_ref[t] == 1)
    def _():
        @pl.when(jnp.logical_and(j == 0, t == 0))
        def _():
            st_s[0] = 0
            prime(te_ref[0], 0)
            convert_chunks(te_ref[0], 0, 0, nc)

        st_s[0] = st_s[0] + 1

        @pl.when(has_next)
        def _():
            prime(nxt_e, nxt_j)

    cur = (st_s[0] + 1) % 2

    @pl.when(tv_ref[t] == 1)
    def _():
        if packed:
            half = a_ref.shape[1]
            x_hi, x_lo = (v.astype(BF16) for v in _unpack_bf16_pair(a_ref[...]))
        else:
            x = a_ref[...]
        ow = cw // 2 if (swiglu or pack_out) else cw
        for c in range(tn // cw):
            cols = pl.ds(c * cw, cw)
            if packed:
                h = (jnp.dot(x_hi, wb_s[cur, pl.ds(0, half), cols], preferred_element_type=F32)
                     + jnp.dot(x_lo, wb_s[cur, pl.ds(half, half), cols], preferred_element_type=F32))
            else:
                h = jnp.dot(x, wb_s[cur, :, cols], preferred_element_type=F32)
            if has_bias:
                h = h + b_ref[:, cols]
            if swiglu:
                h = _swiglu_pairs(h, sel_ref[...])
            if gated_sum:
                h = jax.nn.sigmoid(g1_ref[:, cols]) * m1_ref[:, cols] + jax.nn.sigmoid(g2_ref[:, cols]) * h
            if pack_out:
                o_ref[:, pl.ds(c * ow, ow)] = _pack_bf16_pair(h[:, :ow], h[:, ow:])
            else:
                o_ref[:, pl.ds(c * ow, ow)] = h.astype(o_ref.dtype)

    @pl.when(tv_ref[t] == 0)
    def _():
        o_ref[...] = jnp.zeros_like(o_ref)

    @pl.when(has_next)
    def _():
        convert_chunks(nxt_e, nxt_j, 1 - cur, quota_ref[t])


def _chunk_quota(idx_in_group, tiles_in_group, nc):
    n = jnp.maximum(tiles_in_group, 1)
    return (((idx_in_group + 1) * nc) // n - (idx_in_group * nc) // n).astype(jnp.int32)


def _dense_meta(n_tiles):
    z = jnp.zeros((n_tiles,), jnp.int32)
    one = jnp.ones((n_tiles,), jnp.int32)
    return z, one, z.at[0].set(1), z, one, jnp.arange(n_tiles, dtype=jnp.int32), jnp.full((n_tiles,), n_tiles)


def gmm(a, w, meta, tm, tn, *, bias=None, swiglu=False, packed=False, gated_sum=None, pack_out=False,
        out_dtype=F32):
    r = a.shape[0]
    g, k, n = w.shape
    nj = n // tn
    cw = min(tn, 2 * V7X_MXU_COLS)
    kc = min(k, WEIGHT_CHUNK_ROWS)
    nc = k // kc
    ns = min(nc, WEIGHT_CHUNKS_IN_FLIGHT)
    te, tv, first, nxt, lastg, idx_in_group, tiles_in_group = meta
    quota = jnp.where(tv == 1, _chunk_quota(idx_in_group, tiles_in_group, nc), 0)
    meta = (te, tv, first, nxt, lastg, quota)
    halved = swiglu or pack_out
    ow_total = n // 2 if halved else n
    otn = tn // 2 if halved else tn
    in_specs = [
        pl.BlockSpec((tm, a.shape[1]), lambda j, t, *_: (t, 0)),
        pl.BlockSpec(memory_space=pl.ANY),
    ]
    args = [a, w]
    if bias is not None:
        in_specs.append(pl.BlockSpec((None, 1, tn), lambda j, t, te, *_: (te[t], 0, j)))
        args.append(bias.reshape(g, 1, n))
    if swiglu:
        in_specs.append(pl.BlockSpec((cw, cw // 2), lambda j, t, *_: (0, 0)))
        args.append((jnp.arange(cw)[:, None] == 2 * jnp.arange(cw // 2)[None, :]).astype(BF16))
    if gated_sum is not None:
        m1, gates, col1, col2 = gated_sum
        in_specs += [
            pl.BlockSpec((tm, tn), lambda j, t, *_: (t, j)),
            pl.BlockSpec((tm, tn), lambda j, t, *_: (t, col1 // tn + j)),
            pl.BlockSpec((tm, tn), lambda j, t, *_: (t, col2 // tn + j)),
        ]
        args += [m1, gates, gates]
    kern = functools.partial(_gmm_kernel, tn=tn, nj=nj, cw=cw, kc=kc, nc=nc, ns=ns, packed=packed,
                             has_bias=bias is not None, swiglu=swiglu, gated_sum=gated_sum is not None,
                             pack_out=pack_out)
    grid_spec = pltpu.PrefetchScalarGridSpec(
        num_scalar_prefetch=6,
        grid=(nj, r // tm),
        in_specs=in_specs,
        out_specs=pl.BlockSpec((tm, otn), lambda j, t, *_: (t, j)),
        scratch_shapes=[pltpu.VMEM((ns, kc, tn), F32), pltpu.VMEM((2, k, tn), BF16), pltpu.SMEM((3,), jnp.int32),
                        pltpu.SemaphoreType.DMA((ns,))],
    )
    return pl.pallas_call(
        kern,
        grid_spec=grid_spec,
        out_shape=jax.ShapeDtypeStruct((r, ow_total), jnp.uint32 if pack_out else out_dtype),
        compiler_params=_params("arbitrary", "arbitrary"),
        name="gmm_swiglu" if swiglu else "gmm",
    )(*meta, *args)


def dense(a, w, out_dtype=F32, **kw):
    m, k = a.shape
    pref = 1024 if k <= 4096 else 512
    tm, tn = _tile(m, pref), _tile(w.shape[2], pref)
    return gmm(a, w, _dense_meta(m // tm), tm, tn, out_dtype=out_dtype, **kw)


def _gelu_tanh(x):
    c = math.sqrt(2.0 / math.pi)
    return x * (0.5 * (1.0 + jnp.tanh(c * (x + 0.044715 * (x * x * x)))))


def _softplus(z):
    return jnp.maximum(z, 0.0) + jnp.log1p(jnp.exp(-jnp.abs(z)))


def _lru_kernel(u_ref, g_ref, buf_ref, h0_ref, cw_ref, cb_ref, wa_ref, ba_ref, wx_ref, bx_ref, lam_ref, yin_ref,
                y_ref, nbuf_ref, ht_ref, ext_s, a_s, b_s, hs_s, h_s, *, tt, nblk, bw):
    del yin_ref
    t = pl.program_id(1)
    nt = pl.num_programs(1)
    halo = CONV_W - 1

    @pl.when(t == 0)
    def _():
        ext_s[pl.ds(8 - halo, halo), :] = buf_ref[...]
        h_s[...] = h0_ref[...]

    ext_s[pl.ds(8, tt), :] = u_ref[...]
    uc = cb_ref[...] + sum(ext_s[pl.ds(8 - halo + j, tt), :] * cw_ref[pl.ds(j, 1), :] for j in range(CONV_W))
    tail = ext_s[pl.ds(8 + tt - halo, halo), :]
    ext_s[pl.ds(8 - halo, halo), :] = tail

    sp = _softplus(-lam_ref[...])
    for n in range(nblk):
        cs = slice(n * bw, (n + 1) * bw)
        ub = uc[:, cs]
        ubb = ub.astype(BF16)
        r = jax.nn.sigmoid(jnp.dot(ubb, wa_ref[n].astype(BF16), preferred_element_type=F32) + ba_ref[:, cs])
        i = jax.nn.sigmoid(jnp.dot(ubb, wx_ref[n].astype(BF16), preferred_element_type=F32) + bx_ref[:, cs])
        log_a = -LRU_C * r * sp[:, cs]
        a_s[:, cs] = jnp.exp(log_a)
        one_minus_a2 = -jnp.tanh(log_a) * (jnp.exp(2.0 * log_a) + 1.0)
        b_s[:, cs] = jnp.sqrt(one_minus_a2) * (i * ub)

    def step(s, h):
        h = a_s[pl.ds(s, 1), :] * h + b_s[pl.ds(s, 1), :]
        hs_s[pl.ds(s, 1), :] = h
        return h

    h = lax.fori_loop(0, tt, step, h_s[...], unroll=8)
    h_s[...] = h
    y_ref[...] = (_gelu_tanh(g_ref[...]) * hs_s[...]).astype(y_ref.dtype)

    @pl.when(t == nt - 1)
    def _():
        nbuf_ref[...] = tail
        ht_ref[...] = h


def lru_branch(proj, y_all, row0, nb, t_len, conv_buf, h0, conv_w, conv_b, wa, ba, wx, bx, lam):
    c = conv_w.shape[-1]
    nblk, bw, _ = wa.shape
    tt = _tile(t_len, 256)
    nt = t_len // tt
    rb0 = row0 // tt
    vec = pl.BlockSpec((1, c), lambda b, t: (0, 0))
    kern = functools.partial(_lru_kernel, tt=tt, nblk=nblk, bw=bw)
    return pl.pallas_call(
        kern,
        grid=(nb, nt),
        in_specs=[
            pl.BlockSpec((tt, c), lambda b, t: (rb0 + b * nt + t, 0)),
            pl.BlockSpec((tt, c), lambda b, t: (rb0 + b * nt + t, 1)),
            pl.BlockSpec((None, CONV_W - 1, c), lambda b, t: (b, 0, 0)),
            pl.BlockSpec((None, 1, c), lambda b, t: (b, 0, 0)),
            pl.BlockSpec((CONV_W, c), lambda b, t: (0, 0)),
            vec,
            pl.BlockSpec((nblk, bw, bw), lambda b, t: (0, 0, 0)),
            vec,
            pl.BlockSpec((nblk, bw, bw), lambda b, t: (0, 0, 0)),
            vec,
            vec,
            pl.BlockSpec(memory_space=pl.ANY),
        ],
        out_specs=[
            pl.BlockSpec((tt, c), lambda b, t: (rb0 + b * nt + t, 0)),
            pl.BlockSpec((None, CONV_W - 1, c), lambda b, t: (b, 0, 0)),
            pl.BlockSpec((None, 1, c), lambda b, t: (b, 0, 0)),
        ],
        out_shape=[
            jax.ShapeDtypeStruct(y_all.shape, y_all.dtype),
            jax.ShapeDtypeStruct((nb, CONV_W - 1, c), F32),
            jax.ShapeDtypeStruct((nb, 1, c), F32),
        ],
        scratch_shapes=[
            pltpu.VMEM((tt + 8, c), F32),
            pltpu.VMEM((tt, c), F32),
            pltpu.VMEM((tt, c), F32),
            pltpu.VMEM((tt, c), F32),
            pltpu.VMEM((1, c), F32),
        ],
        input_output_aliases={11: 0},
        compiler_params=_params("parallel", "arbitrary"),
        name="lru_branch",
    )(proj, proj, conv_buf, h0.reshape(nb, 1, c), conv_w, conv_b.reshape(1, c), wa, ba.reshape(1, c), wx,
      bx.reshape(1, c), lam.reshape(1, c), y_all)


def _rotate(x, cos, sin):
    half = x.shape[-1] // 2
    x1, x2 = x[:, :half], x[:, half:]
    return jnp.concatenate([x1 * cos - x2 * sin, x1 * sin + x2 * cos], axis=-1)


def _ret_kernel(q_ref, k_ref, v_ref, g_ref, cos_ref, sin_ref, dec_ref, xi_ref, zeta_ref, gch_ref, s0_ref, ng_ref,
                yin_ref, y_ref, s_ref, *, dk, dv, hb):
    del yin_ref
    c = pl.program_id(2)

    @pl.when(c == 0)
    def _():
        s_ref[...] = s0_ref[...]

    cos, sin = cos_ref[...], sin_ref[...]
    for hh in range(hb):
        kc, vc = pl.ds(hh * dk, dk), pl.ds(hh * dv, dv)
        q = _rotate(q_ref[:, kc], cos, sin)
        k = _rotate(k_ref[:, kc], cos, sin) * (dk ** -0.5)
        vb = v_ref[:, vc].astype(BF16)
        qb = q.astype(BF16)
        s_old = s_ref[hh]
        scores = lax.dot_general(qb, k.astype(BF16), (((1,), (1,)), ((), ())),
                                 preferred_element_type=F32) * dec_ref[hh]
        o = jnp.dot(scores.astype(BF16), vb, preferred_element_type=F32)
        o = o + jnp.dot(qb, s_old.astype(BF16), preferred_element_type=F32) * xi_ref[hh]
        kz = (k * zeta_ref[hh]).astype(BF16)
        s_ref[hh] = gch_ref[hh] * s_old + lax.dot_general(kz, vb, (((0,), (0,)), ((), ())),
                                                          preferred_element_type=F32)
        o = o * lax.rsqrt(jnp.mean(o * o, axis=-1, keepdims=True) + EPS) * ng_ref[hh]
        g = g_ref[:, vc]
        y_ref[:, vc] = (g * jax.nn.sigmoid(g) * o).astype(y_ref.dtype)


def retention_branch(proj, y_all, row0, nb, t_len, pos0, s0, ret_norm_g, col_q, col_k, col_v, col_g):
    _, nh, dk, dv = s0.shape
    ch = RET_CHUNK if t_len % RET_CHUNK == 0 else t_len
    nc = t_len // ch
    rb0 = row0 // ch
    half = dk // 2
    hb = _tile(nh, 4)
    pos = (pos0 + jnp.arange(t_len, dtype=jnp.int32)).astype(F32)
    inv = ROPE_BASE ** (-jnp.arange(half, dtype=F32) / half)
    ang = pos[:, None] * inv[None, :]
    cos, sin = jnp.cos(ang), jnp.sin(ang)
    log_g = jnp.log1p(-jnp.exp2(-5.0 - jnp.arange(nh, dtype=F32)))
    idx = jnp.arange(ch, dtype=F32)
    diff = idx[:, None] - idx[None, :]
    dec = jnp.where(diff[None] >= 0, jnp.exp(jnp.maximum(diff, 0.0)[None] * log_g[:, None, None]), 0.0)
    xi = jnp.exp((idx + 1.0)[None, :, None] * log_g[:, None, None])
    zeta = jnp.exp((ch - 1.0 - idx)[None, :, None] * log_g[:, None, None])
    gch = jnp.exp(ch * log_g)[:, None, None]

    qb, kb, vb, gb = col_q // (hb * dk), col_k // (hb * dk), col_v // (hb * dv), col_g // (hb * dv)
    row = lambda b, h, c: rb0 + b * nc + c
    kern = functools.partial(_ret_kernel, dk=dk, dv=dv, hb=hb)
    return pl.pallas_call(
        kern,
        grid=(nb, nh // hb, nc),
        in_specs=[
            pl.BlockSpec((ch, hb * dk), lambda b, h, c: (row(b, h, c), qb + h)),
            pl.BlockSpec((ch, hb * dk), lambda b, h, c: (row(b, h, c), kb + h)),
            pl.BlockSpec((ch, hb * dv), lambda b, h, c: (row(b, h, c), vb + h)),
            pl.BlockSpec((ch, hb * dv), lambda b, h, c: (row(b, h, c), gb + h)),
            pl.BlockSpec((ch, half), lambda b, h, c: (c, 0)),
            pl.BlockSpec((ch, half), lambda b, h, c: (c, 0)),
            pl.BlockSpec((hb, ch, ch), lambda b, h, c: (h, 0, 0)),
            pl.BlockSpec((hb, ch, 1), lambda b, h, c: (h, 0, 0)),
            pl.BlockSpec((hb, ch, 1), lambda b, h, c: (h, 0, 0)),
            pl.BlockSpec((hb, 1, 1), lambda b, h, c: (h, 0, 0)),
            pl.BlockSpec((None, hb, dk, dv), lambda b, h, c: (b, h, 0, 0)),
            pl.BlockSpec((hb, 1, dv), lambda b, h, c: (h, 0, 0)),
            pl.BlockSpec(memory_space=pl.ANY),
        ],
        out_specs=[
            pl.BlockSpec((ch, hb * dv), lambda b, h, c: (row(b, h, c), h)),
            pl.BlockSpec((None, hb, dk, dv), lambda b, h, c: (b, h, 0, 0)),
        ],
        out_shape=[
            jax.ShapeDtypeStruct(y_all.shape, y_all.dtype),
            jax.ShapeDtypeStruct((nb, nh, dk, dv), F32),
        ],
        input_output_aliases={12: 0},
        compiler_params=_params("parallel", "parallel", "arbitrary"),
        name="retention",
    )(proj, proj, proj, proj, cos, sin, dec, xi, zeta, gch, s0, ret_norm_g.reshape(nh, 1, dv), y_all)


def _xattn_kernel(q_ref, k_ref, v_ref, oin_ref, o_ref, *, scale, nh, hd):
    del oin_ref
    for h in range(nh):
        cols = pl.ds(h * hd, hd)
        kb = k_ref[:, cols].astype(BF16)
        vb = v_ref[:, cols].astype(BF16)
        s = lax.dot_general(q_ref[:, cols], kb, (((1,), (1,)), ((), ())), preferred_element_type=F32) * scale
        e = jnp.exp(s - jnp.max(s, axis=-1, keepdims=True))
        p = e / jnp.sum(e, axis=-1, keepdims=True)
        o_ref[:, cols] = jnp.dot(p.astype(BF16), vb, preferred_element_type=F32).astype(o_ref.dtype)


def cross_attention(q, o_all, row0, nb, t_len, mk, mv, nh):
    d = q.shape[1]
    hd = d // nh
    n_mem = mk.shape[0] // nb
    tq = _tile(t_len, 512)
    nt = t_len // tq
    rb0 = row0 // tq
    kern = functools.partial(_xattn_kernel, scale=hd ** -0.5, nh=nh, hd=hd)
    return pl.pallas_call(
        kern,
        grid=(nb, nt),
        in_specs=[
            pl.BlockSpec((tq, d), lambda b, t: (rb0 + b * nt + t, 0)),
            pl.BlockSpec((n_mem, d), lambda b, t: (b, 0)),
            pl.BlockSpec((n_mem, d), lambda b, t: (b, 0)),
            pl.BlockSpec(memory_space=pl.ANY),
        ],
        out_specs=pl.BlockSpec((tq, d), lambda b, t: (rb0 + b * nt + t, 0)),
        out_shape=jax.ShapeDtypeStruct(o_all.shape, o_all.dtype),
        input_output_aliases={3: 0},
        compiler_params=_params("parallel", "parallel"),
        name="cross_attention",
    )(q, mk, mv, o_all)


def _route_kernel(h_ref, xo_ref, g_ref, rw_ref, rb_ref, h2_ref, hn_ref, ti_ref, pr_ref, rk_ref, cnt_ref, cnt_s,
                  *, tm, ne):
    i = pl.program_id(0)

    @pl.when(i == 0)
    def _():
        cnt_s[...] = jnp.zeros_like(cnt_s)

    h2 = h_ref[...] + xo_ref[...]
    h2_ref[...] = h2
    xn = _rms(h2, g_ref[...])
    half = xn.shape[1] // 2
    hn_ref[...] = _pack_bf16_pair(xn[:, :half], xn[:, half:])
    logits = jnp.dot(xn, rw_ref[...], preferred_element_type=F32, precision=lax.Precision.HIGHEST) + rb_ref[...]
    lane = lax.broadcasted_iota(jnp.int32, (tm, ne), 1)
    vals, idxs, hots = [], [], []
    for _ in range(TOP_K):
        m = jnp.max(logits, axis=-1, keepdims=True)
        idx = jnp.min(jnp.where(logits == m, lane, ne), axis=-1, keepdims=True)
        hot = lane == idx
        vals.append(m)
        idxs.append(idx)
        hots.append(hot.astype(F32))
        logits = jnp.where(hot, -jnp.inf, logits)
    es = [jnp.exp(v - vals[0]) for v in vals]
    den = es[0] + es[1] + es[2] + es[3]
    pr_ref[...] = jnp.concatenate([e / den for e in es], axis=-1)
    ti_ref[...] = jnp.concatenate(idxs, axis=-1)
    oh = hots[0] + hots[1] + hots[2] + hots[3]
    r = lax.broadcasted_iota(jnp.int32, (tm, tm), 0)
    cidx = lax.broadcasted_iota(jnp.int32, (tm, tm), 1)
    tri = (r > cidx).astype(BF16)
    base = cnt_s[...] + jnp.dot(tri, oh.astype(BF16), preferred_element_type=F32)
    rk_ref[...] = jnp.concatenate([jnp.sum(hk * base, axis=-1, keepdims=True) for hk in hots],
                                  axis=-1).astype(jnp.int32)
    cnt = cnt_s[...] + jnp.sum(oh, axis=0, keepdims=True)
    cnt_s[...] = cnt
    cnt_ref[...] = cnt.astype(jnp.int32)


def route(h, xo, g, router_w, router_b):
    n, d = h.shape
    ne = router_w.shape[1]
    tm = _tile(n, 256)
    row = pl.BlockSpec((tm, d), lambda i: (i, 0))
    small = pl.BlockSpec((tm, TOP_K), lambda i: (i, 0))
    kern = functools.partial(_route_kernel, tm=tm, ne=ne)
    return pl.pallas_call(
        kern,
        grid=(n // tm,),
        in_specs=[row, row, pl.BlockSpec((1, d), lambda i: (0, 0)), pl.BlockSpec((d, ne), lambda i: (0, 0)),
                  pl.BlockSpec((1, ne), lambda i: (0, 0))],
        out_specs=[row, pl.BlockSpec((tm, d // 2), lambda i: (i, 0)), small, small, small,
                   pl.BlockSpec((1, ne), lambda i: (0, 0))],
        out_shape=[
            jax.ShapeDtypeStruct((n, d), F32),
            jax.ShapeDtypeStruct((n, d // 2), jnp.uint32),
            jax.ShapeDtypeStruct((n, TOP_K), jnp.int32),
            jax.ShapeDtypeStruct((n, TOP_K), F32),
            jax.ShapeDtypeStruct((n, TOP_K), jnp.int32),
            jax.ShapeDtypeStruct((1, ne), jnp.int32),
        ],
        scratch_shapes=[pltpu.VMEM((1, ne), F32)],
        compiler_params=_params("arbitrary"),
        name="route",
    )(h, xo, g.reshape(1, d), router_w, router_b.reshape(1, ne))


def _dispatch_kernel(pos_hbm, x_ref, xs_in, xs_hbm, pos_s, sem_p, sem, *, tm):
    del xs_in
    i = pl.program_id(0)
    cp = pltpu.make_async_copy(pos_hbm.at[i], pos_s, sem_p)
    cp.start()
    cp.wait()

    def row_copy(r, kk):
        return pltpu.make_async_copy(x_ref.at[pl.ds(r, 1)], xs_hbm.at[pl.ds(pos_s[r * TOP_K + kk], 1)], sem)

    def issue(r, carry):
        for kk in range(TOP_K):
            row_copy(r, kk).start()
        return carry

    lax.fori_loop(0, tm, issue, 0)
    for _ in range(TOP_K):
        pltpu.make_async_copy(x_ref, xs_hbm.at[pl.ds(0, tm)], sem).wait()


def dispatch(x, pos, n_rows):
    n, d = x.shape
    tm = _tile(n, 256)
    kern = functools.partial(_dispatch_kernel, tm=tm)
    return pl.pallas_call(
        kern,
        grid=(n // tm,),
        in_specs=[pl.BlockSpec(memory_space=pl.ANY), pl.BlockSpec((tm, d), lambda i: (i, 0)),
                  pl.BlockSpec(memory_space=pl.ANY)],
        out_specs=pl.BlockSpec(memory_space=pl.ANY),
        out_shape=jax.ShapeDtypeStruct((n_rows, d), x.dtype),
        scratch_shapes=[pltpu.SMEM((tm * TOP_K,), jnp.int32), pltpu.SemaphoreType.DMA, pltpu.SemaphoreType.DMA],
        input_output_aliases={2: 0},
        compiler_params=_params("arbitrary"),
        name="moe_dispatch",
    )(pos.reshape(n // tm, tm * TOP_K), x, jnp.zeros((n_rows, d), x.dtype))


def _combine_kernel(pos_hbm, ys_hbm, h_ref, p_ref, g_ref, o_ref, pos_s, buf, sem_p, sem, *, tm, pw):
    i = pl.program_id(0)
    nt = pl.num_programs(0)
    slot = i % 2

    def fetch(tile, sl):
        cp = pltpu.make_async_copy(pos_hbm.at[tile], pos_s.at[sl], sem_p)
        cp.start()
        cp.wait()

        def issue(r, carry):
            for kk in range(TOP_K):
                pltpu.make_async_copy(ys_hbm.at[pl.ds(pos_s[sl, r * TOP_K + kk], 1)],
                                      buf.at[sl, pl.ds(kk * tm + r, 1)], sem.at[sl]).start()
            return carry

        lax.fori_loop(0, tm, issue, 0)

    @pl.when(i == 0)
    def _():
        fetch(0, 0)

    @pl.when(i + 1 < nt)
    def _():
        fetch(i + 1, 1 - slot)

    pltpu.make_async_copy(ys_hbm.at[pl.ds(0, TOP_K * tm)], buf.at[slot], sem.at[slot]).wait()
    p = p_ref[...]
    parts = []
    for c in range(buf.shape[2] // pw):
        acc_hi = acc_lo = None
        for kk in range(TOP_K):
            hi, lo = _unpack_bf16_pair(buf[slot, pl.ds(kk * tm, tm), pl.ds(c * pw, pw)])
            w = p[:, kk:kk + 1]
            acc_hi = w * hi if acc_hi is None else acc_hi + w * hi
            acc_lo = w * lo if acc_lo is None else acc_lo + w * lo
        parts += [acc_hi, acc_lo]
    o_ref[...] = _rms(h_ref[...] + jnp.concatenate(parts, axis=-1), g_ref[...])


def combine(ys, pos, probs, h, g, pw):
    n, d = h.shape
    tm = _tile(n, 128)
    kern = functools.partial(_combine_kernel, tm=tm, pw=pw)
    row = pl.BlockSpec((tm, d), lambda i: (i, 0))
    return pl.pallas_call(
        kern,
        grid=(n // tm,),
        in_specs=[pl.BlockSpec(memory_space=pl.ANY), pl.BlockSpec(memory_space=pl.ANY), row,
                  pl.BlockSpec((tm, TOP_K), lambda i: (i, 0)), pl.BlockSpec((1, d), lambda i: (0, 0))],
        out_specs=row,
        out_shape=jax.ShapeDtypeStruct((n, d), F32),
        scratch_shapes=[pltpu.SMEM((2, tm * TOP_K), jnp.int32), pltpu.VMEM((2, TOP_K * tm, d // 2), jnp.uint32),
                        pltpu.SemaphoreType.DMA, pltpu.SemaphoreType.DMA((2,))],
        compiler_params=_params("arbitrary"),
        name="moe_combine",
    )(pos.reshape(n // tm, tm * TOP_K), ys, h, probs, g.reshape(1, d))


def _group_tables(counts, top_i, rank, n_rows, tm):
    ne = counts.shape[0]
    padded = (counts + tm - 1) // tm * tm
    ends = jnp.cumsum(padded)
    offs = ends - padded
    pos = offs[top_i] + rank
    tile_start = jnp.arange(n_rows // tm, dtype=jnp.int32) * tm
    te = jnp.sum(tile_start[:, None] >= ends[None, :], axis=1).astype(jnp.int32)
    valid = tile_start < ends[-1]
    eidx = jnp.arange(ne, dtype=jnp.int32)
    nonempty = counts > 0
    first_e = jnp.min(jnp.where(nonempty, eidx, ne))
    last_e = jnp.max(jnp.where(nonempty, eidx, 0))
    later = jnp.logical_and(nonempty[None, :], eidx[None, :] > eidx[:, None])
    nxt_e = jnp.min(jnp.where(later, eidx[None, :], ne), axis=1)
    nxt_e = jnp.where(nxt_e == ne, first_e, nxt_e).astype(jnp.int32)
    te = jnp.where(valid, te, last_e).astype(jnp.int32)
    prev = jnp.concatenate([jnp.full((1,), -1, jnp.int32), te[:-1]])
    first = jnp.logical_and(valid, te != prev)
    tile_idx = jnp.arange(n_rows // tm, dtype=jnp.int32)
    meta = (te, valid.astype(jnp.int32), first.astype(jnp.int32), nxt_e[te],
            (te == last_e).astype(jnp.int32), tile_idx - (offs // tm)[te], (padded // tm)[te])
    return pos, meta


def moe_layer(h, xo, norm_g, router_w, router_b, w_gu, b_gu, w_dn, b_dn, final_g):
    n, d = h.shape
    ne = router_w.shape[1]
    h2, hn, top_i, probs, rank, counts = route(h, xo, norm_g, router_w, router_b)
    tm = 256 if n * TOP_K >= 256 * ne else 8
    n_rows = -(-(n * TOP_K) // tm) * tm + ne * tm
    pos, meta = _group_tables(counts[0], top_i, rank, n_rows, tm)
    xs = dispatch(hn, pos, n_rows)
    act = gmm(xs, w_gu, meta, tm, _tile(w_gu.shape[2], 2048), bias=b_gu, swiglu=True, packed=True, out_dtype=BF16)
    tn = _tile(d, 2048)
    ys = gmm(act, w_dn, meta, tm, tn, bias=b_dn, pack_out=True)
    return combine(ys, pos, probs, h2, final_g, min(tn, 2 * V7X_MXU_COLS) // 2)


def kernel(x_prompt, x_sample, mem_prompt, state_conv, state_lru, state_ret, cache_mem_k, cache_mem_v, norm_mix_g, w_in, conv_w, conv_b, lru_wa, lru_ba, lru_wx, lru_bx, lru_lambda, ret_norm_g, w_lru_branch, w_ret_branch, w_mix_out, norm_xa_g, norm_mem_g, xa_wq, xa_wk, xa_wv, xa_wo, norm_ffn_g, router_w, router_b, moe_w_gu, moe_b_gu, moe_w_dn, moe_b_dn, norm_final_g):
    depth = w_in.shape[0]
    assert depth == 1, "single-layer trunk"
    bp, tp, d = x_prompt.shape
    bs, ts, _ = x_sample.shape
    np_, ns = bp * tp, bs * ts
    n = np_ + ns
    n_mem = mem_prompt.shape[1]
    _, _, nh, dk, dv = state_ret.shape
    xh = cache_mem_k.shape[3]
    c_lru = conv_w.shape[-1]
    l = 0
    sizes = (c_lru, c_lru, nh * dk, nh * dk, nh * dv, nh * dv, d, d)
    cols = [0]
    for s in sizes:
        cols.append(cols[-1] + s)
    c_u, c_gl, c_q, c_k, c_v, c_gr, c_g1, c_g2 = cols[:8]
    assert c_u == 0 and c_gl == c_lru

    xp2, xs2 = x_prompt.reshape(np_, d), x_sample.reshape(ns, d)
    xn = rmsnorm2(xp2, xs2, norm_mix_g[l], BF16)
    proj = dense(xn, w_in)

    lru_args = (conv_w[l], conv_b[l], lru_wa[l], lru_ba[l], lru_wx[l], lru_bx[l], lru_lambda[l])
    y_lru = jnp.zeros((n, c_lru), BF16)
    y_lru, pconv, plru = lru_branch(proj, y_lru, 0, bp, tp, jnp.zeros((bp, CONV_W - 1, c_lru), F32),
                                    jnp.zeros((bp, c_lru), F32), *lru_args)
    y_lru, sconv, slru = lru_branch(proj, y_lru, np_, bs, ts, state_conv[l], state_lru[l], *lru_args)
    y_ret = jnp.zeros((n, nh * dv), BF16)
    y_ret, pret = retention_branch(proj, y_ret, 0, bp, tp, 0, jnp.zeros((bp, nh, dk, dv), F32), ret_norm_g[l],
                                   c_q, c_k, c_v, c_gr)
    y_ret, sret = retention_branch(proj, y_ret, np_, bs, ts, PAST_LEN, state_ret[l], ret_norm_g[l],
                                   c_q, c_k, c_v, c_gr)
    m1 = dense(y_lru, w_lru_branch)
    merged = dense(y_ret, w_ret_branch, BF16, gated_sum=(m1, proj, c_g1, c_g2))
    mix = dense(merged, w_mix_out)
    h, hn = add_rmsnorm2(xp2, xs2, mix, norm_xa_g[l])

    mn = rmsnorm(mem_prompt.reshape(bp * n_mem, d), norm_mem_g[l], BF16)
    mk_p = dense(mn, xa_wk)
    mv_p = dense(mn, xa_wv)
    q = dense(hn, xa_wq, BF16)
    o = jnp.zeros((n, d), BF16)
    o = cross_attention(q, o, 0, bp, tp, mk_p, mv_p, xh)
    o = cross_attention(q, o, np_, bs, ts, cache_mem_k[l].reshape(bs * n_mem, d),
                        cache_mem_v[l].reshape(bs * n_mem, d), xh)
    xo = dense(o, xa_wo)

    y = moe_layer(h, xo, norm_ffn_g[l], router_w[l], router_b[l], moe_w_gu[l], moe_b_gu[l], moe_w_dn[l],
                  moe_b_dn[l], norm_final_g)

    hd = d // xh
    return (y[:np_].reshape(bp, tp, d), y[np_:].reshape(bs, ts, d),
            pconv[None], plru.reshape(1, bp, c_lru), pret[None],
            mk_p.reshape(1, bp, n_mem, xh, hd), mv_p.reshape(1, bp, n_mem, xh, hd),
            sconv[None], slru.reshape(1, bs, c_lru), sret[None])
```

```python
import functools
import math

import jax
import jax.numpy as jnp
from jax import lax
from jax.experimental import pallas as pl
from jax.experimental.pallas import tpu as pltpu

EPS = 1e-6
LRU_C = 8.0
ROPE_BASE = 10000.0
PAST_LEN = 16384
RET_CHUNK = 128
CONV_W = 4
TOP_K = 4
SWIGLU_LIMIT = 7.0
SWIGLU_ALPHA = 1.702

V7X_VMEM_LIMIT_BYTES = 56 * 1024 * 1024
V7X_MXU_COLS = 256
WEIGHT_CHUNK_ROWS = 512
WEIGHT_CHUNKS_IN_FLIGHT = 3
BF16 = jnp.bfloat16
F32 = jnp.float32


def _params(*sem):
    return pltpu.CompilerParams(dimension_semantics=sem, vmem_limit_bytes=V7X_VMEM_LIMIT_BYTES)


def _tile(n, pref):
    t = min(n, pref)
    while n % t:
        t //= 2
    return t


def _rms(x, g):
    return x * lax.rsqrt(jnp.mean(x * x, axis=-1, keepdims=True) + EPS) * g


def _rmsnorm_kernel(x_ref, g_ref, o_ref):
    o_ref[...] = _rms(x_ref[...], g_ref[...]).astype(o_ref.dtype)


def rmsnorm(x, g, out_dtype):
    n, d = x.shape
    tm = _tile(n, 512)
    return pl.pallas_call(
        _rmsnorm_kernel,
        grid=(n // tm,),
        in_specs=[pl.BlockSpec((tm, d), lambda i: (i, 0)), pl.BlockSpec((1, d), lambda i: (0, 0))],
        out_specs=pl.BlockSpec((tm, d), lambda i: (i, 0)),
        out_shape=jax.ShapeDtypeStruct((n, d), out_dtype),
        compiler_params=_params("parallel"),
        name="rmsnorm",
    )(x, g.reshape(1, d))


def _two_group_specs(na_rows, nb_rows, tm, d):
    na, nb = na_rows // tm, nb_rows // tm
    return (na, nb, pl.BlockSpec((tm, d), lambda i: (jnp.minimum(i, na - 1), 0)),
            pl.BlockSpec((tm, d), lambda i: (jnp.maximum(i - na, 0), 0)))


def _rmsnorm2_kernel(xa_ref, xb_ref, g_ref, o_ref, *, na):
    i = pl.program_id(0)

    @pl.when(i < na)
    def _():
        o_ref[...] = _rms(xa_ref[...], g_ref[...]).astype(o_ref.dtype)

    @pl.when(i >= na)
    def _():
        o_ref[...] = _rms(xb_ref[...], g_ref[...]).astype(o_ref.dtype)


def rmsnorm2(xa, xb, g, out_dtype):
    d = xa.shape[1]
    tm = math.gcd(_tile(xa.shape[0], 512), _tile(xb.shape[0], 512))
    na, nb, spec_a, spec_b = _two_group_specs(xa.shape[0], xb.shape[0], tm, d)
    return pl.pallas_call(
        functools.partial(_rmsnorm2_kernel, na=na),
        grid=(na + nb,),
        in_specs=[spec_a, spec_b, pl.BlockSpec((1, d), lambda i: (0, 0))],
        out_specs=pl.BlockSpec((tm, d), lambda i: (i, 0)),
        out_shape=jax.ShapeDtypeStruct((xa.shape[0] + xb.shape[0], d), out_dtype),
        compiler_params=_params("arbitrary"),
        name="rmsnorm2",
    )(xa, xb, g.reshape(1, d))


def _add_rmsnorm2_kernel(xa_ref, xb_ref, y_ref, g_ref, h_ref, hn_ref, *, na):
    i = pl.program_id(0)

    def emit(x_ref):
        h = x_ref[...] + y_ref[...]
        h_ref[...] = h
        hn_ref[...] = _rms(h, g_ref[...]).astype(hn_ref.dtype)

    @pl.when(i < na)
    def _():
        emit(xa_ref)

    @pl.when(i >= na)
    def _():
        emit(xb_ref)


def add_rmsnorm2(xa, xb, y, g):
    d = xa.shape[1]
    tm = math.gcd(_tile(xa.shape[0], 256), _tile(xb.shape[0], 256))
    na, nb, spec_a, spec_b = _two_group_specs(xa.shape[0], xb.shape[0], tm, d)
    n = xa.shape[0] + xb.shape[0]
    row = pl.BlockSpec((tm, d), lambda i: (i, 0))
    return pl.pallas_call(
        functools.partial(_add_rmsnorm2_kernel, na=na),
        grid=(na + nb,),
        in_specs=[spec_a, spec_b, row, pl.BlockSpec((1, d), lambda i: (0, 0))],
        out_specs=[row, row],
        out_shape=[jax.ShapeDtypeStruct((n, d), F32), jax.ShapeDtypeStruct((n, d), BF16)],
        compiler_params=_params("arbitrary"),
        name="add_rmsnorm2",
    )(xa, xb, y, g.reshape(1, d))


def _swiglu_pairs(h, sel):
    gate = jnp.minimum(h, SWIGLU_LIMIT)
    glu = gate * jax.nn.sigmoid(SWIGLU_ALPHA * gate)
    up1 = jnp.clip(h, -SWIGLU_LIMIT, SWIGLU_LIMIT) + 1.0
    prod = (glu * pltpu.roll(up1, h.shape[1] - 1, axis=1)).astype(BF16)
    return jnp.dot(prod, sel, preferred_element_type=F32)


def _pack_bf16_pair(hi, lo):
    hb = lax.bitcast_convert_type(hi.astype(BF16).astype(F32), jnp.uint32)
    lb = lax.bitcast_convert_type(lo.astype(BF16).astype(F32), jnp.uint32)
    return hb | (lb >> 16)


def _unpack_bf16_pair(word):
    return (lax.bitcast_convert_type(word & jnp.uint32(0xFFFF0000), F32),
            lax.bitcast_convert_type(word << 16, F32))


def _gmm_kernel(te_ref, tv_ref, first_ref, nxt_ref, lastg_ref, quota_ref, a_ref, w_hbm, *rest,
                tn, nj, cw, kc, nc, ns, packed, has_bias, swiglu, gated_sum, pack_out):
    rest = list(rest)
    b_ref = rest.pop(0) if has_bias else None
    sel_ref = rest.pop(0) if swiglu else None
    if gated_sum:
        m1_ref, g1_ref, g2_ref = rest.pop(0), rest.pop(0), rest.pop(0)
    o_ref, ring_s, wb_s, st_s, sem = rest
    j = pl.program_id(0)
    t = pl.program_id(1)

    def chunk_copy(e, jj, c):
        slot = c % ns
        return pltpu.make_async_copy(
            w_hbm.at[e, pl.ds(pl.multiple_of(c * kc, kc), kc), pl.ds(pl.multiple_of(jj * tn, tn), tn)],
            ring_s.at[slot], sem.at[slot])

    def prime(e, jj):
        for c in range(ns):
            chunk_copy(e, jj, c).start()
        st_s[1] = ns
        st_s[2] = 0

    def convert_chunks(e, jj, buf, n):
        def body(_, carry):
            c = st_s[2]
            chunk_copy(e, jj, c).wait()
            wb_s[buf, pl.ds(pl.multiple_of(c * kc, kc), kc), :] = ring_s[c % ns].astype(BF16)
            st_s[2] = c + 1
            s = st_s[1]

            @pl.when(s < nc)
            def _():
                chunk_copy(e, jj, s).start()
                st_s[1] = s + 1

            return carry

        lax.fori_loop(0, n, body, 0)

    lastg = lastg_ref[t] == 1
    has_next = jnp.logical_not(jnp.logical_and(lastg, j == nj - 1))
    nxt_e = nxt_ref[t]
    nxt_j = jnp.where(lastg, j + 1, j)

    @pl.when(first_ref[t] == 1)
    def _():
        @pl.when(jnp.logical_and(j == 0, t == 0))
        def _():
            st_s[0] = 0
            prime(te_ref[0], 0)
            convert_chunks(te_ref[0], 0, 0, nc)

        st_s[0] = st_s[0] + 1

        @pl.when(has_next)
        def _():
            prime(nxt_e, nxt_j)

    cur = (st_s[0] + 1) % 2

    @pl.when(tv_ref[t] == 1)
    def _():
        if packed:
            half = a_ref.shape[1]
            x_hi, x_lo = (v.astype(BF16) for v in _unpack_bf16_pair(a_ref[...]))
        else:
            x = a_ref[...]
        ow = cw // 2 if (swiglu or pack_out) else cw
        for c in range(tn // cw):
            cols = pl.ds(c * cw, cw)
            if packed:
                h = (jnp.dot(x_hi, wb_s[cur, pl.ds(0, half), cols], preferred_element_type=F32)
                     + jnp.dot(x_lo, wb_s[cur, pl.ds(half, half), cols], preferred_element_type=F32))
            else:
                h = jnp.dot(x, wb_s[cur, :, cols], preferred_element_type=F32)
            if has_bias:
                h = h + b_ref[:, cols]
            if swiglu:
                h = _swiglu_pairs(h, sel_ref[...])
            if gated_sum:
                h = jax.nn.sigmoid(g1_ref[:, cols]) * m1_ref[:, cols] + jax.nn.sigmoid(g2_ref[:, cols]) * h
            if pack_out:
                o_ref[:, pl.ds(c * ow, ow)] = _pack_bf16_pair(h[:, :ow], h[:, ow:])
            else:
                o_ref[:, pl.ds(c * ow, ow)] = h.astype(o_ref.dtype)

    @pl.when(tv_ref[t] == 0)
    def _():
        o_ref[...] = jnp.zeros_like(o_ref)

    @pl.when(has_next)
    def _():
        convert_chunks(nxt_e, nxt_j, 1 - cur, quota_ref[t])


def _chunk_quota(idx_in_group, tiles_in_group, nc):
    n = jnp.maximum(tiles_in_group, 1)
    return (((idx_in_group + 1) * nc) // n - (idx_in_group * nc) // n).astype(jnp.int32)


def _dense_meta(n_tiles):
    z = jnp.zeros((n_tiles,), jnp.int32)
    one = jnp.ones((n_tiles,), jnp.int32)
    return z, one, z.at[0].set(1), z, one, jnp.arange(n_tiles, dtype=jnp.int32), jnp.full((n_tiles,), n_tiles)


def gmm(a, w, meta, tm, tn, *, bias=None, swiglu=False, packed=False, gated_sum=None, pack_out=False,
        out_dtype=F32):
    r = a.shape[0]
    g, k, n = w.shape
    nj = n // tn
    cw = min(tn, 2 * V7X_MXU_COLS)
    kc = min(k, WEIGHT_CHUNK_ROWS)
    nc = k // kc
    ns = min(nc, WEIGHT_CHUNKS_IN_FLIGHT)
    te, tv, first, nxt, lastg, idx_in_group, tiles_in_group = meta
    quota = jnp.where(tv == 1, _chunk_quota(idx_in_group, tiles_in_group, nc), 0)
    meta = (te, tv, first, nxt, lastg, quota)
    halved = swiglu or pack_out
    ow_total = n // 2 if halved else n
    otn = tn // 2 if halved else tn
    in_specs = [
        pl.BlockSpec((tm, a.shape[1]), lambda j, t, *_: (t, 0)),
        pl.BlockSpec(memory_space=pl.ANY),
    ]
    args = [a, w]
    if bias is not None:
        in_specs.append(pl.BlockSpec((None, 1, tn), lambda j, t, te, *_: (te[t], 0, j)))
        args.append(bias.reshape(g, 1, n))
    if swiglu:
        in_specs.append(pl.BlockSpec((cw, cw // 2), lambda j, t, *_: (0, 0)))
        args.append((jnp.arange(cw)[:, None] == 2 * jnp.arange(cw // 2)[None, :]).astype(BF16))
    if gated_sum is not None:
        m1, gates, col1, col2 = gated_sum
        in_specs += [
            pl.BlockSpec((tm, tn), lambda j, t, *_: (t, j)),
            pl.BlockSpec((tm, tn), lambda j, t, *_: (t, col1 // tn + j)),
            pl.BlockSpec((tm, tn), lambda j, t, *_: (t, col2 // tn + j)),
        ]
        args += [m1, gates, gates]
    kern = functools.partial(_gmm_kernel, tn=tn, nj=nj, cw=cw, kc=kc, nc=nc, ns=ns, packed=packed,
                             has_bias=bias is not None, swiglu=swiglu, gated_sum=gated_sum is not None,
                             pack_out=pack_out)
    grid_spec = pltpu.PrefetchScalarGridSpec(
        num_scalar_prefetch=6,
        grid=(nj, r // tm),
        in_specs=in_specs,
        out_specs=pl.BlockSpec((tm, otn), lambda j, t, *_: (t, j)),
        scratch_shapes=[pltpu.VMEM((ns, kc, tn), F32), pltpu.VMEM((2, k, tn), BF16), pltpu.SMEM((3,), jnp.int32),
                        pltpu.SemaphoreType.DMA((ns,))],
    )
    return pl.pallas_call(
        kern,
        grid_spec=grid_spec,
        out_shape=jax.ShapeDtypeStruct((r, ow_total), jnp.uint32 if pack_out else out_dtype),
        compiler_params=_params("arbitrary", "arbitrary"),
        name="gmm_swiglu" if swiglu else "gmm",
    )(*meta, *args)


def dense(a, w, out_dtype=F32, **kw):
    m, k = a.shape
    pref = 1024 if k <= 4096 else 512
    tm, tn = _tile(m, pref), _tile(w.shape[2], pref)
    return gmm(a, w, _dense_meta(m // tm), tm, tn, out_dtype=out_dtype, **kw)


def _gelu_tanh(x):
    c = math.sqrt(2.0 / math.pi)
    return x * (0.5 * (1.0 + jnp.tanh(c * (x + 0.044715 * (x * x * x)))))


def _softplus(z):
    return jnp.maximum(z, 0.0) + jnp.log1p(jnp.exp(-jnp.abs(z)))


def _lru_kernel(u_ref, g_ref, buf_ref, h0_ref, cw_ref, cb_ref, wa_ref, ba_ref, wx_ref, bx_ref, lam_ref, yin_ref,
                y_ref, nbuf_ref, ht_ref, ext_s, a_s, b_s, hs_s, h_s, *, tt, nblk, bw):
    del yin_ref
    t = pl.program_id(1)
    nt = pl.num_programs(1)
    halo = CONV_W - 1

    @pl.when(t == 0)
    def _():
        ext_s[pl.ds(8 - halo, halo), :] = buf_ref[...]
        h_s[...] = h0_ref[...]

    ext_s[pl.ds(8, tt), :] = u_ref[...]
    uc = cb_ref[...] + sum(ext_s[pl.ds(8 - halo + j, tt), :] * cw_ref[pl.ds(j, 1), :] for j in range(CONV_W))
    tail = ext_s[pl.ds(8 + tt - halo, halo), :]
    ext_s[pl.ds(8 - halo, halo), :] = tail

    sp = _softplus(-lam_ref[...])
    for n in range(nblk):
        cs = slice(n * bw, (n + 1) * bw)
        ub = uc[:, cs]
        ubb = ub.astype(BF16)
        r = jax.nn.sigmoid(jnp.dot(ubb, wa_ref[n].astype(BF16), preferred_element_type=F32) + ba_ref[:, cs])
        i = jax.nn.sigmoid(jnp.dot(ubb, wx_ref[n].astype(BF16), preferred_element_type=F32) + bx_ref[:, cs])
        log_a = -LRU_C * r * sp[:, cs]
        a_s[:, cs] = jnp.exp(log_a)
        one_minus_a2 = -jnp.tanh(log_a) * (jnp.exp(2.0 * log_a) + 1.0)
        b_s[:, cs] = jnp.sqrt(one_minus_a2) * (i * ub)

    def step(s, h):
        h = a_s[pl.ds(s, 1), :] * h + b_s[pl.ds(s, 1), :]
        hs_s[pl.ds(s, 1), :] = h
        return h

    h = lax.fori_loop(0, tt, step, h_s[...], unroll=8)
    h_s[...] = h
    y_ref[...] = (_gelu_tanh(g_ref[...]) * hs_s[...]).astype(y_ref.dtype)

    @pl.when(t == nt - 1)
    def _():
        nbuf_ref[...] = tail
        ht_ref[...] = h


def lru_branch(proj, y_all, row0, nb, t_len, conv_buf, h0, conv_w, conv_b, wa, ba, wx, bx, lam):
    c = conv_w.shape[-1]
    nblk, bw, _ = wa.shape
    tt = _tile(t_len, 256)
    nt = t_len // tt
    rb0 = row0 // tt
    vec = pl.BlockSpec((1, c), lambda b, t: (0, 0))
    kern = functools.partial(_lru_kernel, tt=tt, nblk=nblk, bw=bw)
    return pl.pallas_call(
        kern,
        grid=(nb, nt),
        in_specs=[
            pl.BlockSpec((tt, c), lambda b, t: (rb0 + b * nt + t, 0)),
            pl.BlockSpec((tt, c), lambda b, t: (rb0 + b * nt + t, 1)),
            pl.BlockSpec((None, CONV_W - 1, c), lambda b, t: (b, 0, 0)),
            pl.BlockSpec((None, 1, c), lambda b, t: (b, 0, 0)),
            pl.BlockSpec((CONV_W, c), lambda b, t: (0, 0)),
            vec,
            pl.BlockSpec((nblk, bw, bw), lambda b, t: (0, 0, 0)),
            vec,
            pl.BlockSpec((nblk, bw, bw), lambda b, t: (0, 0, 0)),
            vec,
            vec,
            pl.BlockSpec(memory_space=pl.ANY),
        ],
        out_specs=[
            pl.BlockSpec((tt, c), lambda b, t: (rb0 + b * nt + t, 0)),
            pl.BlockSpec((None, CONV_W - 1, c), lambda b, t: (b, 0, 0)),
            pl.BlockSpec((None, 1, c), lambda b, t: (b, 0, 0)),
        ],
        out_shape=[
            jax.ShapeDtypeStruct(y_all.shape, y_all.dtype),
            jax.ShapeDtypeStruct((nb, CONV_W - 1, c), F32),
            jax.ShapeDtypeStruct((nb, 1, c), F32),
        ],
        scratch_shapes=[
            pltpu.VMEM((tt + 8, c), F32),
            pltpu.VMEM((tt, c), F32),
            pltpu.VMEM((tt, c), F32),
            pltpu.VMEM((tt, c), F32),
            pltpu.VMEM((1, c), F32),
        ],
        input_output_aliases={11: 0},
        compiler_params=_params("parallel", "arbitrary"),
        name="lru_branch",
    )(proj, proj, conv_buf, h0.reshape(nb, 1, c), conv_w, conv_b.reshape(1, c), wa, ba.reshape(1, c), wx,
      bx.reshape(1, c), lam.reshape(1, c), y_all)


def _rotate(x, cos, sin):
    half = x.shape[-1] // 2
    x1, x2 = x[:, :half], x[:, half:]
    return jnp.concatenate([x1 * cos - x2 * sin, x1 * sin + x2 * cos], axis=-1)


def _ret_kernel(q_ref, k_ref, v_ref, g_ref, cos_ref, sin_ref, dec_ref, xi_ref, zeta_ref, gch_ref, s0_ref, ng_ref,
                yin_ref, y_ref, s_ref, *, dk, dv, hb):
    del yin_ref
    c = pl.program_id(2)

    @pl.when(c == 0)
    def _():
        s_ref[...] = s0_ref[...]

    cos, sin = cos_ref[...], sin_ref[...]
    for hh in range(hb):
        kc, vc = pl.ds(hh * dk, dk), pl.ds(hh * dv, dv)
        q = _rotate(q_ref[:, kc], cos, sin)
        k = _rotate(k_ref[:, kc], cos, sin) * (dk ** -0.5)
        vb = v_ref[:, vc].astype(BF16)
        qb = q.astype(BF16)
        s_old = s_ref[hh]
        scores = lax.dot_general(qb, k.astype(BF16), (((1,), (1,)), ((), ())),
                                 preferred_element_type=F32) * dec_ref[hh]
        o = jnp.dot(scores.astype(BF16), vb, preferred_element_type=F32)
        o = o + jnp.dot(qb, s_old.astype(BF16), preferred_element_type=F32) * xi_ref[hh]
        kz = (k * zeta_ref[hh]).astype(BF16)
        s_ref[hh] = gch_ref[hh] * s_old + lax.dot_general(kz, vb, (((0,), (0,)), ((), ())),
                                                          preferred_element_type=F32)
        o = o * lax.rsqrt(jnp.mean(o * o, axis=-1, keepdims=True) + EPS) * ng_ref[hh]
        g = g_ref[:, vc]
        y_ref[:, vc] = (g * jax.nn.sigmoid(g) * o).astype(y_ref.dtype)


def retention_branch(proj, y_all, row0, nb, t_len, pos0, s0, ret_norm_g, col_q, col_k, col_v, col_g):
    _, nh, dk, dv = s0.shape
    ch = RET_CHUNK if t_len % RET_CHUNK == 0 else t_len
    nc = t_len // ch
    rb0 = row0 // ch
    half = dk // 2
    hb = _tile(nh, 4)
    pos = (pos0 + jnp.arange(t_len, dtype=jnp.int32)).astype(F32)
    inv = ROPE_BASE ** (-jnp.arange(half, dtype=F32) / half)
    ang = pos[:, None] * inv[None, :]
    cos, sin = jnp.cos(ang), jnp.sin(ang)
    log_g = jnp.log1p(-jnp.exp2(-5.0 - jnp.arange(nh, dtype=F32)))
    idx = jnp.arange(ch, dtype=F32)
    diff = idx[:, None] - idx[None, :]
    dec = jnp.where(diff[None] >= 0, jnp.exp(jnp.maximum(diff, 0.0)[None] * log_g[:, None, None]), 0.0)
    xi = jnp.exp((idx + 1.0)[None, :, None] * log_g[:, None, None])
    zeta = jnp.exp((ch - 1.0 - idx)[None, :, None] * log_g[:, None, None])
    gch = jnp.exp(ch * log_g)[:, None, None]

    qb, kb, vb, gb = col_q // (hb * dk), col_k // (hb * dk), col_v // (hb * dv), col_g // (hb * dv)
    row = lambda b, h, c: rb0 + b * nc + c
    kern = functools.partial(_ret_kernel, dk=dk, dv=dv, hb=hb)
    return pl.pallas_call(
        kern,
        grid=(nb, nh // hb, nc),
        in_specs=[
            pl.BlockSpec((ch, hb * dk), lambda b, h, c: (row(b, h, c), qb + h)),
            pl.BlockSpec((ch, hb * dk), lambda b, h, c: (row(b, h, c), kb + h)),
            pl.BlockSpec((ch, hb * dv), lambda b, h, c: (row(b, h, c), vb + h)),
            pl.BlockSpec((ch, hb * dv), lambda b, h, c: (row(b, h, c), gb + h)),
            pl.BlockSpec((ch, half), lambda b, h, c: (c, 0)),
            pl.BlockSpec((ch, half), lambda b, h, c: (c, 0)),
            pl.BlockSpec((hb, ch, ch), lambda b, h, c: (h, 0, 0)),
            pl.BlockSpec((hb, ch, 1), lambda b, h, c: (h, 0, 0)),
            pl.BlockSpec((hb, ch, 1), lambda b, h, c: (h, 0, 0)),
            pl.BlockSpec((hb, 1, 1), lambda b, h, c: (h, 0, 0)),
            pl.BlockSpec((None, hb, dk, dv), lambda b, h, c: (b, h, 0, 0)),
            pl.BlockSpec((hb, 1, dv), lambda b, h, c: (h, 0, 0)),
            pl.BlockSpec(memory_space=pl.ANY),
        ],
        out_specs=[
            pl.BlockSpec((ch, hb * dv), lambda b, h, c: (row(b, h, c), h)),
            pl.BlockSpec((None, hb, dk, dv), lambda b, h, c: (b, h, 0, 0)),
        ],
        out_shape=[
            jax.ShapeDtypeStruct(y_all.shape, y_all.dtype),
            jax.ShapeDtypeStruct((nb, nh, dk, dv), F32),
        ],
        input_output_aliases={12: 0},
        compiler_params=_params("parallel", "parallel", "arbitrary"),
        name="retention",
    )(proj, proj, proj, proj, cos, sin, dec, xi, zeta, gch, s0, ret_norm_g.reshape(nh, 1, dv), y_all)


def _xattn_kernel(q_ref, k_ref, v_ref, oin_ref, o_ref, *, scale, nh, hd):
    del oin_ref
    for h in range(nh):
        cols = pl.ds(h * hd, hd)
        kb = k_ref[:, cols].astype(BF16)
        vb = v_ref[:, cols].astype(BF16)
        s = lax.dot_general(q_ref[:, cols], kb, (((1,), (1,)), ((), ())), preferred_element_type=F32) * scale
        e = jnp.exp(s - jnp.max(s, axis=-1, keepdims=True))
        p = e / jnp.sum(e, axis=-1, keepdims=True)
        o_ref[:, cols] = jnp.dot(p.astype(BF16), vb, preferred_element_type=F32).astype(o_ref.dtype)


def cross_attention(q, o_all, row0, nb, t_len, mk, mv, nh):
    d = q.shape[1]
    hd = d // nh
    n_mem = mk.shape[0] // nb
    tq = _tile(t_len, 512)
    nt = t_len // tq
    rb0 = row0 // tq
    kern = functools.partial(_xattn_kernel, scale=hd ** -0.5, nh=nh, hd=hd)
    return pl.pallas_call(
        kern,
        grid=(nb, nt),
        in_specs=[
            pl.BlockSpec((tq, d), lambda b, t: (rb0 + b * nt + t, 0)),
            pl.BlockSpec((n_mem, d), lambda b, t: (b, 0)),
            pl.BlockSpec((n_mem, d), lambda b, t: (b, 0)),
            pl.BlockSpec(memory_space=pl.ANY),
        ],
        out_specs=pl.BlockSpec((tq, d), lambda b, t: (rb0 + b * nt + t, 0)),
        out_shape=jax.ShapeDtypeStruct(o_all.shape, o_all.dtype),
        input_output_aliases={3: 0},
        compiler_params=_params("parallel", "parallel"),
        name="cross_attention",
    )(q, mk, mv, o_all)


def _route_kernel(h_ref, xo_ref, g_ref, rw_ref, rb_ref, h2_ref, hn_ref, ti_ref, pr_ref, rk_ref, cnt_ref, cnt_s,
                  *, tm, ne):
    i = pl.program_id(0)

    @pl.when(i == 0)
    def _():
        cnt_s[...] = jnp.zeros_like(cnt_s)

    h2 = h_ref[...] + xo_ref[...]
    h2_ref[...] = h2
    xn = _rms(h2, g_ref[...])
    half = xn.shape[1] // 2
    hn_ref[...] = _pack_bf16_pair(xn[:, :half], xn[:, half:])
    logits = jnp.dot(xn, rw_ref[...], preferred_element_type=F32, precision=lax.Precision.HIGHEST) + rb_ref[...]
    lane = lax.broadcasted_iota(jnp.int32, (tm, ne), 1)
    vals, idxs, hots = [], [], []
    for _ in range(TOP_K):
        m = jnp.max(logits, axis=-1, keepdims=True)
        idx = jnp.min(jnp.where(logits == m, lane, ne), axis=-1, keepdims=True)
        hot = lane == idx
        vals.append(m)
        idxs.append(idx)
        hots.append(hot.astype(F32))
        logits = jnp.where(hot, -jnp.inf, logits)
    es = [jnp.exp(v - vals[0]) for v in vals]
    den = es[0] + es[1] + es[2] + es[3]
    pr_ref[...] = jnp.concatenate([e / den for e in es], axis=-1)
    ti_ref[...] = jnp.concatenate(idxs, axis=-1)
    oh = hots[0] + hots[1] + hots[2] + hots[3]
    r = lax.broadcasted_iota(jnp.int32, (tm, tm), 0)
    cidx = lax.broadcasted_iota(jnp.int32, (tm, tm), 1)
    tri = (r > cidx).astype(BF16)
    base = cnt_s[...] + jnp.dot(tri, oh.astype(BF16), preferred_element_type=F32)
    rk_ref[...] = jnp.concatenate([jnp.sum(hk * base, axis=-1, keepdims=True) for hk in hots],
                                  axis=-1).astype(jnp.int32)
    cnt = cnt_s[...] + jnp.sum(oh, axis=0, keepdims=True)
    cnt_s[...] = cnt
    cnt_ref[...] = cnt.astype(jnp.int32)


def route(h, xo, g, router_w, router_b):
    n, d = h.shape
    ne = router_w.shape[1]
    tm = _tile(n, 256)
    row = pl.BlockSpec((tm, d), lambda i: (i, 0))
    small = pl.BlockSpec((tm, TOP_K), lambda i: (i, 0))
    kern = functools.partial(_route_kernel, tm=tm, ne=ne)
    return pl.pallas_call(
        kern,
        grid=(n // tm,),
        in_specs=[row, row, pl.BlockSpec((1, d), lambda i: (0, 0)), pl.BlockSpec((d, ne), lambda i: (0, 0)),
                  pl.BlockSpec((1, ne), lambda i: (0, 0))],
        out_specs=[row, pl.BlockSpec((tm, d // 2), lambda i: (i, 0)), small, small, small,
                   pl.BlockSpec((1, ne), lambda i: (0, 0))],
        out_shape=[
            jax.ShapeDtypeStruct((n, d), F32),
            jax.ShapeDtypeStruct((n, d // 2), jnp.uint32),
            jax.ShapeDtypeStruct((n, TOP_K), jnp.int32),
            jax.ShapeDtypeStruct((n, TOP_K), F32),
            jax.ShapeDtypeStruct((n, TOP_K), jnp.int32),
            jax.ShapeDtypeStruct((1, ne), jnp.int32),
        ],
        scratch_shapes=[pltpu.VMEM((1, ne), F32)],
        compiler_params=_params("arbitrary"),
        name="route",
    )(h, xo, g.reshape(1, d), router_w, router_b.reshape(1, ne))


def _dispatch_kernel(pos_hbm, x_ref, xs_in, xs_hbm, pos_s, sem_p, sem, *, tm):
    del xs_in
    i = pl.program_id(0)
    cp = pltpu.make_async_copy(pos_hbm.at[i], pos_s, sem_p)
    cp.start()
    cp.wait()

    def row_copy(r, kk):
        return pltpu.make_async_copy(x_ref.at[pl.ds(r, 1)], xs_hbm.at[pl.ds(pos_s[r * TOP_K + kk], 1)], sem)

    def issue(r, carry):
        for kk in range(TOP_K):
            row_copy(r, kk).start()
        return carry

    lax.fori_loop(0, tm, issue, 0)
    for _ in range(TOP_K):
        pltpu.make_async_copy(x_ref, xs_hbm.at[pl.ds(0, tm)], sem).wait()


def dispatch(x, pos, n_rows):
    n, d = x.shape
    tm = _tile(n, 256)
    kern = functools.partial(_dispatch_kernel, tm=tm)
    return pl.pallas_call(
        kern,
        grid=(n // tm,),
        in_specs=[pl.BlockSpec(memory_space=pl.ANY), pl.BlockSpec((tm, d), lambda i: (i, 0)),
                  pl.BlockSpec(memory_space=pl.ANY)],
        out_specs=pl.BlockSpec(memory_space=pl.ANY),
        out_shape=jax.ShapeDtypeStruct((n_rows, d), x.dtype),
        scratch_shapes=[pltpu.SMEM((tm * TOP_K,), jnp.int32), pltpu.SemaphoreType.DMA, pltpu.SemaphoreType.DMA],
        input_output_aliases={2: 0},
        compiler_params=_params("arbitrary"),
        name="moe_dispatch",
    )(pos.reshape(n // tm, tm * TOP_K), x, jnp.zeros((n_rows, d), x.dtype))


def _combine_kernel(pos_hbm, ys_hbm, h_ref, p_ref, g_ref, o_ref, pos_s, buf, sem_p, sem, *, tm, pw, t0):
    i = pl.program_id(0)
    nt = pl.num_programs(0)
    slot = i % 2

    def fetch(tile, sl):
        cp = pltpu.make_async_copy(pos_hbm.at[t0 + tile], pos_s.at[sl], sem_p)
        cp.start()
        cp.wait()

        def issue(rb, carry):
            r0 = pl.multiple_of(rb * 8, 8)
            for s in range(8):
                for kk in range(TOP_K):
                    src_row = pos_s[sl, (r0 + s) * TOP_K + kk]
                    pltpu.make_async_copy(ys_hbm.at[pl.ds(src_row, 1)],
                                          buf.at[sl, pl.ds(kk * tm + r0 + s, 1)], sem.at[sl]).start()
            return carry

        lax.fori_loop(0, tm // 8, issue, 0)

    @pl.when(i == 0)
    def _():
        fetch(0, 0)

    for sl in (0, 1):
        @pl.when(jnp.logical_and(i + 1 < nt, slot != sl))
        def _(sl=sl):
            fetch(i + 1, sl)

    pltpu.make_async_copy(ys_hbm.at[pl.ds(0, TOP_K * tm)], buf.at[slot], sem.at[slot]).wait()
    p = p_ref[...]
    parts = []
    for c in range(buf.shape[2] // pw):
        acc_hi = acc_lo = None
        for kk in range(TOP_K):
            hi, lo = _unpack_bf16_pair(buf[slot, pl.ds(kk * tm, tm), pl.ds(c * pw, pw)])
            w = p[:, kk:kk + 1]
            acc_hi = w * hi if acc_hi is None else acc_hi + w * hi
            acc_lo = w * lo if acc_lo is None else acc_lo + w * lo
        parts += [acc_hi, acc_lo]
    o_ref[...] = _rms(h_ref[...] + jnp.concatenate(parts, axis=-1), g_ref[...])


def combine(ys, pos, probs, h, g, pw, tm, row0, nrows):
    n, d = h.shape
    t0 = row0 // tm
    kern = functools.partial(_combine_kernel, tm=tm, pw=pw, t0=t0)
    return pl.pallas_call(
        kern,
        grid=(nrows // tm,),
        in_specs=[pl.BlockSpec(memory_space=pl.ANY), pl.BlockSpec(memory_space=pl.ANY),
                  pl.BlockSpec((tm, d), lambda i: (t0 + i, 0)),
                  pl.BlockSpec((tm, TOP_K), lambda i: (t0 + i, 0)), pl.BlockSpec((1, d), lambda i: (0, 0))],
        out_specs=pl.BlockSpec((tm, d), lambda i: (i, 0)),
        out_shape=jax.ShapeDtypeStruct((nrows, d), F32),
        scratch_shapes=[pltpu.SMEM((2, tm * TOP_K), jnp.int32), pltpu.VMEM((2, TOP_K * tm, d // 2), jnp.uint32),
                        pltpu.SemaphoreType.DMA, pltpu.SemaphoreType.DMA((2,))],
        compiler_params=_params("arbitrary"),
        name="moe_combine",
    )(pos.reshape(n // tm, tm * TOP_K), ys, h, probs, g.reshape(1, d))


def _group_tables(counts, top_i, rank, n_rows, tm):
    ne = counts.shape[0]
    padded = (counts + tm - 1) // tm * tm
    ends = jnp.cumsum(padded)
    offs = ends - padded
    pos = offs[top_i] + rank
    tile_start = jnp.arange(n_rows // tm, dtype=jnp.int32) * tm
    te = jnp.sum(tile_start[:, None] >= ends[None, :], axis=1).astype(jnp.int32)
    valid = tile_start < ends[-1]
    eidx = jnp.arange(ne, dtype=jnp.int32)
    nonempty = counts > 0
    first_e = jnp.min(jnp.where(nonempty, eidx, ne))
    last_e = jnp.max(jnp.where(nonempty, eidx, 0))
    later = jnp.logical_and(nonempty[None, :], eidx[None, :] > eidx[:, None])
    nxt_e = jnp.min(jnp.where(later, eidx[None, :], ne), axis=1)
    nxt_e = jnp.where(nxt_e == ne, first_e, nxt_e).astype(jnp.int32)
    te = jnp.where(valid, te, last_e).astype(jnp.int32)
    prev = jnp.concatenate([jnp.full((1,), -1, jnp.int32), te[:-1]])
    first = jnp.logical_and(valid, te != prev)
    tile_idx = jnp.arange(n_rows // tm, dtype=jnp.int32)
    meta = (te, valid.astype(jnp.int32), first.astype(jnp.int32), nxt_e[te],
            (te == last_e).astype(jnp.int32), tile_idx - (offs // tm)[te], (padded // tm)[te])
    return pos, meta


def moe_layer(h, xo, norm_g, router_w, router_b, w_gu, b_gu, w_dn, b_dn, final_g, n_first):
    n, d = h.shape
    ne = router_w.shape[1]
    h2, hn, top_i, probs, rank, counts = route(h, xo, norm_g, router_w, router_b)
    tm = 256 if n * TOP_K >= 256 * ne else 8
    n_rows = -(-(n * TOP_K) // tm) * tm + ne * tm
    pos, meta = _group_tables(counts[0], top_i, rank, n_rows, tm)
    xs = dispatch(hn, pos, n_rows)
    act = gmm(xs, w_gu, meta, tm, _tile(w_gu.shape[2], 2048), bias=b_gu, swiglu=True, packed=True, out_dtype=BF16)
    tn = _tile(d, 2048)
    ys = gmm(act, w_dn, meta, tm, tn, bias=b_dn, pack_out=True)
    pw = min(tn, 2 * V7X_MXU_COLS) // 2
    tm_c = math.gcd(_tile(n_first, 128), _tile(n - n_first, 128))
    return (combine(ys, pos, probs, h2, final_g, pw, tm_c, 0, n_first),
            combine(ys, pos, probs, h2, final_g, pw, tm_c, n_first, n - n_first))


def kernel(x_prompt, x_sample, mem_prompt, state_conv, state_lru, state_ret, cache_mem_k, cache_mem_v, norm_mix_g, w_in, conv_w, conv_b, lru_wa, lru_ba, lru_wx, lru_bx, lru_lambda, ret_norm_g, w_lru_branch, w_ret_branch, w_mix_out, norm_xa_g, norm_mem_g, xa_wq, xa_wk, xa_wv, xa_wo, norm_ffn_g, router_w, router_b, moe_w_gu, moe_b_gu, moe_w_dn, moe_b_dn, norm_final_g):
    depth = w_in.shape[0]
    assert depth == 1, "single-layer trunk"
    bp, tp, d = x_prompt.shape
    bs, ts, _ = x_sample.shape
    np_, ns = bp * tp, bs * ts
    n = np_ + ns
    n_mem = mem_prompt.shape[1]
    _, _, nh, dk, dv = state_ret.shape
    xh = cache_mem_k.shape[3]
    c_lru = conv_w.shape[-1]
    l = 0
    sizes = (c_lru, c_lru, nh * dk, nh * dk, nh * dv, nh * dv, d, d)
    cols = [0]
    for s in sizes:
        cols.append(cols[-1] + s)
    c_u, c_gl, c_q, c_k, c_v, c_gr, c_g1, c_g2 = cols[:8]
    assert c_u == 0 and c_gl == c_lru

    xp2, xs2 = x_prompt.reshape(np_, d), x_sample.reshape(ns, d)
    xn = rmsnorm2(xp2, xs2, norm_mix_g[l], BF16)
    proj = dense(xn, w_in)

    lru_args = (conv_w[l], conv_b[l], lru_wa[l], lru_ba[l], lru_wx[l], lru_bx[l], lru_lambda[l])
    y_lru = jnp.zeros((n, c_lru), BF16)
    y_lru, pconv, plru = lru_branch(proj, y_lru, 0, bp, tp, jnp.zeros((bp, CONV_W - 1, c_lru), F32),
                                    jnp.zeros((bp, c_lru), F32), *lru_args)
    y_lru, sconv, slru = lru_branch(proj, y_lru, np_, bs, ts, state_conv[l], state_lru[l], *lru_args)
    y_ret = jnp.zeros((n, nh * dv), BF16)
    y_ret, pret = retention_branch(proj, y_ret, 0, bp, tp, 0, jnp.zeros((bp, nh, dk, dv), F32), ret_norm_g[l],
                                   c_q, c_k, c_v, c_gr)
    y_ret, sret = retention_branch(proj, y_ret, np_, bs, ts, PAST_LEN, state_ret[l], ret_norm_g[l],
                                   c_q, c_k, c_v, c_gr)
    m1 = dense(y_lru, w_lru_branch)
    merged = dense(y_ret, w_ret_branch, BF16, gated_sum=(m1, proj, c_g1, c_g2))
    mix = dense(merged, w_mix_out)
    h, hn = add_rmsnorm2(xp2, xs2, mix, norm_xa_g[l])

    mn = rmsnorm(mem_prompt.reshape(bp * n_mem, d), norm_mem_g[l], BF16)
    mk_p = dense(mn, xa_wk)
    mv_p = dense(mn, xa_wv)
    q = dense(hn, xa_wq, BF16)
    o = jnp.zeros((n, d), BF16)
    o = cross_attention(q, o, 0, bp, tp, mk_p, mv_p, xh)
    o = cross_attention(q, o, np_, bs, ts, cache_mem_k[l].reshape(bs * n_mem, d),
                        cache_mem_v[l].reshape(bs * n_mem, d), xh)
    xo = dense(o, xa_wo)

    y_p, y_s = moe_layer(h, xo, norm_ffn_g[l], router_w[l], router_b[l], moe_w_gu[l], moe_b_gu[l], moe_w_dn[l],
                         moe_b_dn[l], norm_final_g, np_)

    hd = d // xh
    return (y_p.reshape(bp, tp, d), y_s.reshape(bs, ts, d),
            pconv[None], plru.reshape(1, bp, c_lru), pret[None],
            mk_p.reshape(1, bp, n_mem, xh, hd), mv_p.reshape(1, bp, n_mem, xh, hd),
            sconv[None], slru.reshape(1, bs, c_lru), sret[None])
```

```python
import functools
import math

import jax
import jax.numpy as jnp
from jax import lax
from jax.experimental import pallas as pl
from jax.experimental.pallas import tpu as pltpu

EPS = 1e-6
LRU_C = 8.0
ROPE_BASE = 10000.0
PAST_LEN = 16384
RET_CHUNK = 128
CONV_W = 4
TOP_K = 4
SWIGLU_LIMIT = 7.0
SWIGLU_ALPHA = 1.702

V7X_VMEM_LIMIT_BYTES = 56 * 1024 * 1024
V7X_MXU_COLS = 256
WEIGHT_CHUNK_ROWS = 512
WEIGHT_CHUNKS_IN_FLIGHT = 3
BF16 = jnp.bfloat16
F32 = jnp.float32


def _params(*sem):
    return pltpu.CompilerParams(dimension_semantics=sem, vmem_limit_bytes=V7X_VMEM_LIMIT_BYTES)


def _tile(n, pref):
    t = min(n, pref)
    while n % t:
        t //= 2
    return t


def _rms(x, g):
    return x * lax.rsqrt(jnp.mean(x * x, axis=-1, keepdims=True) + EPS) * g


def _rmsnorm_kernel(x_ref, g_ref, o_ref):
    o_ref[...] = _rms(x_ref[...], g_ref[...]).astype(o_ref.dtype)


def rmsnorm(x, g, out_dtype):
    n, d = x.shape
    tm = _tile(n, 512)
    return pl.pallas_call(
        _rmsnorm_kernel,
        grid=(n // tm,),
        in_specs=[pl.BlockSpec((tm, d), lambda i: (i, 0)), pl.BlockSpec((1, d), lambda i: (0, 0))],
        out_specs=pl.BlockSpec((tm, d), lambda i: (i, 0)),
        out_shape=jax.ShapeDtypeStruct((n, d), out_dtype),
        compiler_params=_params("parallel"),
        name="rmsnorm",
    )(x, g.reshape(1, d))


def _two_group_specs(na_rows, nb_rows, tm, d):
    na, nb = na_rows // tm, nb_rows // tm
    return (na, nb, pl.BlockSpec((tm, d), lambda i: (jnp.minimum(i, na - 1), 0)),
            pl.BlockSpec((tm, d), lambda i: (jnp.maximum(i - na, 0), 0)))


def _rmsnorm2_kernel(xa_ref, xb_ref, g_ref, o_ref, *, na):
    i = pl.program_id(0)

    @pl.when(i < na)
    def _():
        o_ref[...] = _rms(xa_ref[...], g_ref[...]).astype(o_ref.dtype)

    @pl.when(i >= na)
    def _():
        o_ref[...] = _rms(xb_ref[...], g_ref[...]).astype(o_ref.dtype)


def rmsnorm2(xa, xb, g, out_dtype):
    d = xa.shape[1]
    tm = math.gcd(_tile(xa.shape[0], 512), _tile(xb.shape[0], 512))
    na, nb, spec_a, spec_b = _two_group_specs(xa.shape[0], xb.shape[0], tm, d)
    return pl.pallas_call(
        functools.partial(_rmsnorm2_kernel, na=na),
        grid=(na + nb,),
        in_specs=[spec_a, spec_b, pl.BlockSpec((1, d), lambda i: (0, 0))],
        out_specs=pl.BlockSpec((tm, d), lambda i: (i, 0)),
        out_shape=jax.ShapeDtypeStruct((xa.shape[0] + xb.shape[0], d), out_dtype),
        compiler_params=_params("arbitrary"),
        name="rmsnorm2",
    )(xa, xb, g.reshape(1, d))


def _add_rmsnorm2_kernel(xa_ref, xb_ref, y_ref, g_ref, h_ref, hn_ref, *, na):
    i = pl.program_id(0)

    def emit(x_ref):
        h = x_ref[...] + y_ref[...]
        h_ref[...] = h
        hn_ref[...] = _rms(h, g_ref[...]).astype(hn_ref.dtype)

    @pl.when(i < na)
    def _():
        emit(xa_ref)

    @pl.when(i >= na)
    def _():
        emit(xb_ref)


def add_rmsnorm2(xa, xb, y, g):
    d = xa.shape[1]
    tm = math.gcd(_tile(xa.shape[0], 256), _tile(xb.shape[0], 256))
    na, nb, spec_a, spec_b = _two_group_specs(xa.shape[0], xb.shape[0], tm, d)
    n = xa.shape[0] + xb.shape[0]
    row = pl.BlockSpec((tm, d), lambda i: (i, 0))
    return pl.pallas_call(
        functools.partial(_add_rmsnorm2_kernel, na=na),
        grid=(na + nb,),
        in_specs=[spec_a, spec_b, row, pl.BlockSpec((1, d), lambda i: (0, 0))],
        out_specs=[row, row],
        out_shape=[jax.ShapeDtypeStruct((n, d), F32), jax.ShapeDtypeStruct((n, d), BF16)],
        compiler_params=_params("arbitrary"),
        name="add_rmsnorm2",
    )(xa, xb, y, g.reshape(1, d))


def _swiglu_pairs(h, sel):
    gate = jnp.minimum(h, SWIGLU_LIMIT)
    glu = gate * jax.nn.sigmoid(SWIGLU_ALPHA * gate)
    up1 = jnp.clip(h, -SWIGLU_LIMIT, SWIGLU_LIMIT) + 1.0
    prod = (glu * pltpu.roll(up1, h.shape[1] - 1, axis=1)).astype(BF16)
    return jnp.dot(prod, sel, preferred_element_type=F32)


def _pack_bf16_pair(hi, lo):
    hb = lax.bitcast_convert_type(hi.astype(BF16).astype(F32), jnp.uint32)
    lb = lax.bitcast_convert_type(lo.astype(BF16).astype(F32), jnp.uint32)
    return hb | (lb >> 16)


def _unpack_bf16_pair(word):
    return (lax.bitcast_convert_type(word & jnp.uint32(0xFFFF0000), F32),
            lax.bitcast_convert_type(word << 16, F32))


def _gmm_kernel(te_ref, tv_ref, first_ref, nxt_ref, lastg_ref, quota_ref, a_ref, w_hbm, *rest,
                tn, nj, cw, kc, nc, ns, packed, has_bias, swiglu, gated_sum, pack_out):
    rest = list(rest)
    b_ref = rest.pop(0) if has_bias else None
    sel_ref = rest.pop(0) if swiglu else None
    if gated_sum:
        m1_ref, g1_ref, g2_ref = rest.pop(0), rest.pop(0), rest.pop(0)
    o_ref, ring_s, wb_s, st_s, sem = rest
    j = pl.program_id(0)
    t = pl.program_id(1)

    def chunk_copy(e, jj, c):
        slot = c % ns
        return pltpu.make_async_copy(
            w_hbm.at[e, pl.ds(pl.multiple_of(c * kc, kc), kc), pl.ds(pl.multiple_of(jj * tn, tn), tn)],
            ring_s.at[slot], sem.at[slot])

    def prime(e, jj):
        for c in range(ns):
            chunk_copy(e, jj, c).start()
        st_s[1] = ns
        st_s[2] = 0

    def convert_chunks(e, jj, buf, n):
        def body(_, carry):
            c = st_s[2]
            chunk_copy(e, jj, c).wait()
            wb_s[buf, pl.ds(pl.multiple_of(c * kc, kc), kc), :] = ring_s[c % ns].astype(BF16)
            st_s[2] = c + 1
            s = st_s[1]

            @pl.when(s < nc)
            def _():
                chunk_copy(e, jj, s).start()
                st_s[1] = s + 1

            return carry

        lax.fori_loop(0, n, body, 0)

    lastg = lastg_ref[t] == 1
    has_next = jnp.logical_not(jnp.logical_and(lastg, j == nj - 1))
    nxt_e = nxt_ref[t]
    nxt_j = jnp.where(lastg, j + 1, j)

    @pl.when(first_ref[t] == 1)
    def _():
        @pl.when(jnp.logical_and(j == 0, t == 0))
        def _():
            st_s[0] = 0
            prime(te_ref[0], 0)
            convert_chunks(te_ref[0], 0, 0, nc)

        st_s[0] = st_s[0] + 1

        @pl.when(has_next)
        def _():
            prime(nxt_e, nxt_j)

    cur = (st_s[0] + 1) % 2

    @pl.when(tv_ref[t] == 1)
    def _():
        if packed:
            half = a_ref.shape[1]
            x_hi, x_lo = (v.astype(BF16) for v in _unpack_bf16_pair(a_ref[...]))
        else:
            x = a_ref[...]
        ow = cw // 2 if (swiglu or pack_out) else cw
        for c in range(tn // cw):
            cols = pl.ds(c * cw, cw)
            if packed:
                h = (jnp.dot(x_hi, wb_s[cur, pl.ds(0, half), cols], preferred_element_type=F32)
                     + jnp.dot(x_lo, wb_s[cur, pl.ds(half, half), cols], preferred_element_type=F32))
            else:
                h = jnp.dot(x, wb_s[cur, :, cols], preferred_element_type=F32)
            if has_bias:
                h = h + b_ref[:, cols]
            if swiglu:
                h = _swiglu_pairs(h, sel_ref[...])
            if gated_sum:
                h = jax.nn.sigmoid(g1_ref[:, cols]) * m1_ref[:, cols] + jax.nn.sigmoid(g2_ref[:, cols]) * h
            if pack_out:
                o_ref[:, pl.ds(c * ow, ow)] = _pack_bf16_pair(h[:, :ow], h[:, ow:])
            else:
                o_ref[:, pl.ds(c * ow, ow)] = h.astype(o_ref.dtype)

    @pl.when(tv_ref[t] == 0)
    def _():
        o_ref[...] = jnp.zeros_like(o_ref)

    @pl.when(has_next)
    def _():
        convert_chunks(nxt_e, nxt_j, 1 - cur, quota_ref[t])


def _chunk_quota(idx_in_group, tiles_in_group, nc):
    n = jnp.maximum(tiles_in_group, 1)
    return (((idx_in_group + 1) * nc) // n - (idx_in_group * nc) // n).astype(jnp.int32)


def _dense_meta(n_tiles):
    z = jnp.zeros((n_tiles,), jnp.int32)
    one = jnp.ones((n_tiles,), jnp.int32)
    return z, one, z.at[0].set(1), z, one, jnp.arange(n_tiles, dtype=jnp.int32), jnp.full((n_tiles,), n_tiles)


def gmm(a, w, meta, tm, tn, *, bias=None, swiglu=False, packed=False, gated_sum=None, pack_out=False,
        out_dtype=F32):
    r = a.shape[0]
    g, k, n = w.shape
    nj = n // tn
    cw = min(tn, 2 * V7X_MXU_COLS)
    kc = min(k, WEIGHT_CHUNK_ROWS)
    nc = k // kc
    ns = min(nc, WEIGHT_CHUNKS_IN_FLIGHT)
    te, tv, first, nxt, lastg, idx_in_group, tiles_in_group = meta
    quota = jnp.where(tv == 1, _chunk_quota(idx_in_group, tiles_in_group, nc), 0)
    meta = (te, tv, first, nxt, lastg, quota)
    halved = swiglu or pack_out
    ow_total = n // 2 if halved else n
    otn = tn // 2 if halved else tn
    in_specs = [
        pl.BlockSpec((tm, a.shape[1]), lambda j, t, *_: (t, 0)),
        pl.BlockSpec(memory_space=pl.ANY),
    ]
    args = [a, w]
    if bias is not None:
        in_specs.append(pl.BlockSpec((None, 1, tn), lambda j, t, te, *_: (te[t], 0, j)))
        args.append(bias.reshape(g, 1, n))
    if swiglu:
        in_specs.append(pl.BlockSpec((cw, cw // 2), lambda j, t, *_: (0, 0)))
        args.append((jnp.arange(cw)[:, None] == 2 * jnp.arange(cw // 2)[None, :]).astype(BF16))
    if gated_sum is not None:
        m1, gates, col1, col2 = gated_sum
        in_specs += [
            pl.BlockSpec((tm, tn), lambda j, t, *_: (t, j)),
            pl.BlockSpec((tm, tn), lambda j, t, *_: (t, col1 // tn + j)),
            pl.BlockSpec((tm, tn), lambda j, t, *_: (t, col2 // tn + j)),
        ]
        args += [m1, gates, gates]
    kern = functools.partial(_gmm_kernel, tn=tn, nj=nj, cw=cw, kc=kc, nc=nc, ns=ns, packed=packed,
                             has_bias=bias is not None, swiglu=swiglu, gated_sum=gated_sum is not None,
                             pack_out=pack_out)
    grid_spec = pltpu.PrefetchScalarGridSpec(
        num_scalar_prefetch=6,
        grid=(nj, r // tm),
        in_specs=in_specs,
        out_specs=pl.BlockSpec((tm, otn), lambda j, t, *_: (t, j)),
        scratch_shapes=[pltpu.VMEM((ns, kc, tn), F32), pltpu.VMEM((2, k, tn), BF16), pltpu.SMEM((3,), jnp.int32),
                        pltpu.SemaphoreType.DMA((ns,))],
    )
    return pl.pallas_call(
        kern,
        grid_spec=grid_spec,
        out_shape=jax.ShapeDtypeStruct((r, ow_total), jnp.uint32 if pack_out else out_dtype),
        compiler_params=_params("arbitrary", "arbitrary"),
        name="gmm_swiglu" if swiglu else "gmm",
    )(*meta, *args)


def dense(a, w, out_dtype=F32, **kw):
    m, k = a.shape
    pref = 1024 if k <= 4096 else 512
    tm, tn = _tile(m, pref), _tile(w.shape[2], pref)
    return gmm(a, w, _dense_meta(m // tm), tm, tn, out_dtype=out_dtype, **kw)


def _gelu_tanh(x):
    c = math.sqrt(2.0 / math.pi)
    return x * (0.5 * (1.0 + jnp.tanh(c * (x + 0.044715 * (x * x * x)))))


def _softplus(z):
    return jnp.maximum(z, 0.0) + jnp.log1p(jnp.exp(-jnp.abs(z)))


def _lru_kernel(u_ref, g_ref, buf_ref, h0_ref, cw_ref, cb_ref, wa_ref, ba_ref, wx_ref, bx_ref, lam_ref, yin_ref,
                y_ref, nbuf_ref, ht_ref, ext_s, a_s, b_s, hs_s, h_s, *, tt, nblk, bw):
    del yin_ref
    t = pl.program_id(1)
    nt = pl.num_programs(1)
    halo = CONV_W - 1

    @pl.when(t == 0)
    def _():
        ext_s[pl.ds(8 - halo, halo), :] = buf_ref[...]
        h_s[...] = h0_ref[...]

    ext_s[pl.ds(8, tt), :] = u_ref[...]
    uc = cb_ref[...] + sum(ext_s[pl.ds(8 - halo + j, tt), :] * cw_ref[pl.ds(j, 1), :] for j in range(CONV_W))
    tail = ext_s[pl.ds(8 + tt - halo, halo), :]
    ext_s[pl.ds(8 - halo, halo), :] = tail

    sp = _softplus(-lam_ref[...])
    for n in range(nblk):
        cs = slice(n * bw, (n + 1) * bw)
        ub = uc[:, cs]
        ubb = ub.astype(BF16)
        r = jax.nn.sigmoid(jnp.dot(ubb, wa_ref[n].astype(BF16), preferred_element_type=F32) + ba_ref[:, cs])
        i = jax.nn.sigmoid(jnp.dot(ubb, wx_ref[n].astype(BF16), preferred_element_type=F32) + bx_ref[:, cs])
        log_a = -LRU_C * r * sp[:, cs]
        a_s[:, cs] = jnp.exp(log_a)
        one_minus_a2 = -jnp.tanh(log_a) * (jnp.exp(2.0 * log_a) + 1.0)
        b_s[:, cs] = jnp.sqrt(one_minus_a2) * (i * ub)

    def step(s, h):
        h = a_s[pl.ds(s, 1), :] * h + b_s[pl.ds(s, 1), :]
        hs_s[pl.ds(s, 1), :] = h
        return h

    h = lax.fori_loop(0, tt, step, h_s[...], unroll=8)
    h_s[...] = h
    y_ref[...] = (_gelu_tanh(g_ref[...]) * hs_s[...]).astype(y_ref.dtype)

    @pl.when(t == nt - 1)
    def _():
        nbuf_ref[...] = tail
        ht_ref[...] = h


def lru_branch(proj, y_all, row0, nb, t_len, conv_buf, h0, conv_w, conv_b, wa, ba, wx, bx, lam):
    c = conv_w.shape[-1]
    nblk, bw, _ = wa.shape
    tt = _tile(t_len, 256)
    nt = t_len // tt
    rb0 = row0 // tt
    vec = pl.BlockSpec((1, c), lambda b, t: (0, 0))
    kern = functools.partial(_lru_kernel, tt=tt, nblk=nblk, bw=bw)
    return pl.pallas_call(
        kern,
        grid=(nb, nt),
        in_specs=[
            pl.BlockSpec((tt, c), lambda b, t: (rb0 + b * nt + t, 0)),
            pl.BlockSpec((tt, c), lambda b, t: (rb0 + b * nt + t, 1)),
            pl.BlockSpec((None, CONV_W - 1, c), lambda b, t: (b, 0, 0)),
            pl.BlockSpec((None, 1, c), lambda b, t: (b, 0, 0)),
            pl.BlockSpec((CONV_W, c), lambda b, t: (0, 0)),
            vec,
            pl.BlockSpec((nblk, bw, bw), lambda b, t: (0, 0, 0)),
            vec,
            pl.BlockSpec((nblk, bw, bw), lambda b, t: (0, 0, 0)),
            vec,
            vec,
            pl.BlockSpec(memory_space=pl.ANY),
        ],
        out_specs=[
            pl.BlockSpec((tt, c), lambda b, t: (rb0 + b * nt + t, 0)),
            pl.BlockSpec((None, CONV_W - 1, c), lambda b, t: (b, 0, 0)),
            pl.BlockSpec((None, 1, c), lambda b, t: (b, 0, 0)),
        ],
        out_shape=[
            jax.ShapeDtypeStruct(y_all.shape, y_all.dtype),
            jax.ShapeDtypeStruct((nb, CONV_W - 1, c), F32),
            jax.ShapeDtypeStruct((nb, 1, c), F32),
        ],
        scratch_shapes=[
            pltpu.VMEM((tt + 8, c), F32),
            pltpu.VMEM((tt, c), F32),
            pltpu.VMEM((tt, c), F32),
            pltpu.VMEM((tt, c), F32),
            pltpu.VMEM((1, c), F32),
        ],
        input_output_aliases={11: 0},
        compiler_params=_params("parallel", "arbitrary"),
        name="lru_branch",
    )(proj, proj, conv_buf, h0.reshape(nb, 1, c), conv_w, conv_b.reshape(1, c), wa, ba.reshape(1, c), wx,
      bx.reshape(1, c), lam.reshape(1, c), y_all)


def _rotate(x, cos, sin):
    half = x.shape[-1] // 2
    x1, x2 = x[:, :half], x[:, half:]
    return jnp.concatenate([x1 * cos - x2 * sin, x1 * sin + x2 * cos], axis=-1)


def _ret_kernel(q_ref, k_ref, v_ref, g_ref, cos_ref, sin_ref, dec_ref, xi_ref, zeta_ref, gch_ref, s0_ref, ng_ref,
                yin_ref, y_ref, s_ref, *, dk, dv, hb):
    del yin_ref
    c = pl.program_id(2)

    @pl.when(c == 0)
    def _():
        s_ref[...] = s0_ref[...]

    cos, sin = cos_ref[...], sin_ref[...]
    for hh in range(hb):
        kc, vc = pl.ds(hh * dk, dk), pl.ds(hh * dv, dv)
        q = _rotate(q_ref[:, kc], cos, sin)
        k = _rotate(k_ref[:, kc], cos, sin) * (dk ** -0.5)
        vb = v_ref[:, vc].astype(BF16)
        qb = q.astype(BF16)
        s_old = s_ref[hh]
        scores = lax.dot_general(qb, k.astype(BF16), (((1,), (1,)), ((), ())),
                                 preferred_element_type=F32) * dec_ref[hh]
        o = jnp.dot(scores.astype(BF16), vb, preferred_element_type=F32)
        o = o + jnp.dot(qb, s_old.astype(BF16), preferred_element_type=F32) * xi_ref[hh]
        kz = (k * zeta_ref[hh]).astype(BF16)
        s_ref[hh] = gch_ref[hh] * s_old + lax.dot_general(kz, vb, (((0,), (0,)), ((), ())),
                                                          preferred_element_type=F32)
        o = o * lax.rsqrt(jnp.mean(o * o, axis=-1, keepdims=True) + EPS) * ng_ref[hh]
        g = g_ref[:, vc]
        y_ref[:, vc] = (g * jax.nn.sigmoid(g) * o).astype(y_ref.dtype)


def retention_branch(proj, y_all, row0, nb, t_len, pos0, s0, ret_norm_g, col_q, col_k, col_v, col_g):
    _, nh, dk, dv = s0.shape
    ch = RET_CHUNK if t_len % RET_CHUNK == 0 else t_len
    nc = t_len // ch
    rb0 = row0 // ch
    half = dk // 2
    hb = _tile(nh, 4)
    pos = (pos0 + jnp.arange(t_len, dtype=jnp.int32)).astype(F32)
    inv = ROPE_BASE ** (-jnp.arange(half, dtype=F32) / half)
    ang = pos[:, None] * inv[None, :]
    cos, sin = jnp.cos(ang), jnp.sin(ang)
    log_g = jnp.log1p(-jnp.exp2(-5.0 - jnp.arange(nh, dtype=F32)))
    idx = jnp.arange(ch, dtype=F32)
    diff = idx[:, None] - idx[None, :]
    dec = jnp.where(diff[None] >= 0, jnp.exp(jnp.maximum(diff, 0.0)[None] * log_g[:, None, None]), 0.0)
    xi = jnp.exp((idx + 1.0)[None, :, None] * log_g[:, None, None])
    zeta = jnp.exp((ch - 1.0 - idx)[None, :, None] * log_g[:, None, None])
    gch = jnp.exp(ch * log_g)[:, None, None]

    qb, kb, vb, gb = col_q // (hb * dk), col_k // (hb * dk), col_v // (hb * dv), col_g // (hb * dv)
    row = lambda b, h, c: rb0 + b * nc + c
    kern = functools.partial(_ret_kernel, dk=dk, dv=dv, hb=hb)
    return pl.pallas_call(
        kern,
        grid=(nb, nh // hb, nc),
        in_specs=[
            pl.BlockSpec((ch, hb * dk), lambda b, h, c: (row(b, h, c), qb + h)),
            pl.BlockSpec((ch, hb * dk), lambda b, h, c: (row(b, h, c), kb + h)),
            pl.BlockSpec((ch, hb * dv), lambda b, h, c: (row(b, h, c), vb + h)),
            pl.BlockSpec((ch, hb * dv), lambda b, h, c: (row(b, h, c), gb + h)),
            pl.BlockSpec((ch, half), lambda b, h, c: (c, 0)),
            pl.BlockSpec((ch, half), lambda b, h, c: (c, 0)),
            pl.BlockSpec((hb, ch, ch), lambda b, h, c: (h, 0, 0)),
            pl.BlockSpec((hb, ch, 1), lambda b, h, c: (h, 0, 0)),
            pl.BlockSpec((hb, ch, 1), lambda b, h, c: (h, 0, 0)),
            pl.BlockSpec((hb, 1, 1), lambda b, h, c: (h, 0, 0)),
            pl.BlockSpec((None, hb, dk, dv), lambda b, h, c: (b, h, 0, 0)),
            pl.BlockSpec((hb, 1, dv), lambda b, h, c: (h, 0, 0)),
            pl.BlockSpec(memory_space=pl.ANY),
        ],
        out_specs=[
            pl.BlockSpec((ch, hb * dv), lambda b, h, c: (row(b, h, c), h)),
            pl.BlockSpec((None, hb, dk, dv), lambda b, h, c: (b, h, 0, 0)),
        ],
        out_shape=[
            jax.ShapeDtypeStruct(y_all.shape, y_all.dtype),
            jax.ShapeDtypeStruct((nb, nh, dk, dv), F32),
        ],
        input_output_aliases={12: 0},
        compiler_params=_params("parallel", "parallel", "arbitrary"),
        name="retention",
    )(proj, proj, proj, proj, cos, sin, dec, xi, zeta, gch, s0, ret_norm_g.reshape(nh, 1, dv), y_all)


def _xattn_kernel(q_ref, k_ref, v_ref, oin_ref, o_ref, *, scale, nh, hd):
    del oin_ref
    for h in range(nh):
        cols = pl.ds(h * hd, hd)
        kb = k_ref[:, cols].astype(BF16)
        vb = v_ref[:, cols].astype(BF16)
        s = lax.dot_general(q_ref[:, cols], kb, (((1,), (1,)), ((), ())), preferred_element_type=F32) * scale
        e = jnp.exp(s - jnp.max(s, axis=-1, keepdims=True))
        p = e / jnp.sum(e, axis=-1, keepdims=True)
        o_ref[:, cols] = jnp.dot(p.astype(BF16), vb, preferred_element_type=F32).astype(o_ref.dtype)


def cross_attention(q, o_all, row0, nb, t_len, mk, mv, nh):
    d = q.shape[1]
    hd = d // nh
    n_mem = mk.shape[0] // nb
    tq = _tile(t_len, 512)
    nt = t_len // tq
    rb0 = row0 // tq
    kern = functools.partial(_xattn_kernel, scale=hd ** -0.5, nh=nh, hd=hd)
    return pl.pallas_call(
        kern,
        grid=(nb, nt),
        in_specs=[
            pl.BlockSpec((tq, d), lambda b, t: (rb0 + b * nt + t, 0)),
            pl.BlockSpec((n_mem, d), lambda b, t: (b, 0)),
            pl.BlockSpec((n_mem, d), lambda b, t: (b, 0)),
            pl.BlockSpec(memory_space=pl.ANY),
        ],
        out_specs=pl.BlockSpec((tq, d), lambda b, t: (rb0 + b * nt + t, 0)),
        out_shape=jax.ShapeDtypeStruct(o_all.shape, o_all.dtype),
        input_output_aliases={3: 0},
        compiler_params=_params("parallel", "parallel"),
        name="cross_attention",
    )(q, mk, mv, o_all)


def _route_kernel(h_ref, xo_ref, g_ref, rw_ref, rb_ref, h2_ref, hn_ref, ti_ref, pr_ref, rk_ref, cnt_ref, cnt_s,
                  *, tm, ne):
    i = pl.program_id(0)

    @pl.when(i == 0)
    def _():
        cnt_s[...] = jnp.zeros_like(cnt_s)

    h2 = h_ref[...] + xo_ref[...]
    h2_ref[...] = h2
    xn = _rms(h2, g_ref[...])
    half = xn.shape[1] // 2
    hn_ref[...] = _pack_bf16_pair(xn[:, :half], xn[:, half:])
    logits = jnp.dot(xn, rw_ref[...], preferred_element_type=F32, precision=lax.Precision.HIGHEST) + rb_ref[...]
    lane = lax.broadcasted_iota(jnp.int32, (tm, ne), 1)
    vals, idxs, hots = [], [], []
    for _ in range(TOP_K):
        m = jnp.max(logits, axis=-1, keepdims=True)
        idx = jnp.min(jnp.where(logits == m, lane, ne), axis=-1, keepdims=True)
        hot = lane == idx
        vals.append(m)
        idxs.append(idx)
        hots.append(hot.astype(F32))
        logits = jnp.where(hot, -jnp.inf, logits)
    es = [jnp.exp(v - vals[0]) for v in vals]
    den = es[0] + es[1] + es[2] + es[3]
    pr_ref[...] = jnp.concatenate([e / den for e in es], axis=-1)
    ti_ref[...] = jnp.concatenate(idxs, axis=-1)
    oh = hots[0] + hots[1] + hots[2] + hots[3]
    r = lax.broadcasted_iota(jnp.int32, (tm, tm), 0)
    cidx = lax.broadcasted_iota(jnp.int32, (tm, tm), 1)
    tri = (r > cidx).astype(BF16)
    base = cnt_s[...] + jnp.dot(tri, oh.astype(BF16), preferred_element_type=F32)
    rk_ref[...] = jnp.concatenate([jnp.sum(hk * base, axis=-1, keepdims=True) for hk in hots],
                                  axis=-1).astype(jnp.int32)
    cnt = cnt_s[...] + jnp.sum(oh, axis=0, keepdims=True)
    cnt_s[...] = cnt
    cnt_ref[...] = cnt.astype(jnp.int32)


def route(h, xo, g, router_w, router_b):
    n, d = h.shape
    ne = router_w.shape[1]
    tm = _tile(n, 256)
    row = pl.BlockSpec((tm, d), lambda i: (i, 0))
    small = pl.BlockSpec((tm, TOP_K), lambda i: (i, 0))
    kern = functools.partial(_route_kernel, tm=tm, ne=ne)
    return pl.pallas_call(
        kern,
        grid=(n // tm,),
        in_specs=[row, row, pl.BlockSpec((1, d), lambda i: (0, 0)), pl.BlockSpec((d, ne), lambda i: (0, 0)),
                  pl.BlockSpec((1, ne), lambda i: (0, 0))],
        out_specs=[row, pl.BlockSpec((tm, d // 2), lambda i: (i, 0)), small, small, small,
                   pl.BlockSpec((1, ne), lambda i: (0, 0))],
        out_shape=[
            jax.ShapeDtypeStruct((n, d), F32),
            jax.ShapeDtypeStruct((n, d // 2), jnp.uint32),
            jax.ShapeDtypeStruct((n, TOP_K), jnp.int32),
            jax.ShapeDtypeStruct((n, TOP_K), F32),
            jax.ShapeDtypeStruct((n, TOP_K), jnp.int32),
            jax.ShapeDtypeStruct((1, ne), jnp.int32),
        ],
        scratch_shapes=[pltpu.VMEM((1, ne), F32)],
        compiler_params=_params("arbitrary"),
        name="route",
    )(h, xo, g.reshape(1, d), router_w, router_b.reshape(1, ne))


def _dispatch_kernel(pos_hbm, x_ref, xs_in, xs_hbm, pos_s, sem_p, sem, *, tm):
    del xs_in
    i = pl.program_id(0)
    cp = pltpu.make_async_copy(pos_hbm.at[i], pos_s, sem_p)
    cp.start()
    cp.wait()

    def row_copy(r, kk):
        return pltpu.make_async_copy(x_ref.at[pl.ds(r, 1)], xs_hbm.at[pl.ds(pos_s[r * TOP_K + kk], 1)], sem)

    def issue(r, carry):
        for kk in range(TOP_K):
            row_copy(r, kk).start(priority=kk % 2)
        return carry

    lax.fori_loop(0, tm, issue, 0)
    for _ in range(TOP_K):
        pltpu.make_async_copy(x_ref, xs_hbm.at[pl.ds(0, tm)], sem).wait()


def dispatch(x, pos, n_rows):
    n, d = x.shape
    tm = _tile(n, 256)
    kern = functools.partial(_dispatch_kernel, tm=tm)
    return pl.pallas_call(
        kern,
        grid=(n // tm,),
        in_specs=[pl.BlockSpec(memory_space=pl.ANY), pl.BlockSpec((tm, d), lambda i: (i, 0)),
                  pl.BlockSpec(memory_space=pl.ANY)],
        out_specs=pl.BlockSpec(memory_space=pl.ANY),
        out_shape=jax.ShapeDtypeStruct((n_rows, d), x.dtype),
        scratch_shapes=[pltpu.SMEM((tm * TOP_K,), jnp.int32), pltpu.SemaphoreType.DMA, pltpu.SemaphoreType.DMA],
        input_output_aliases={2: 0},
        compiler_params=_params("arbitrary"),
        name="moe_dispatch",
    )(pos.reshape(n // tm, tm * TOP_K), x, jnp.zeros((n_rows, d), x.dtype))


def _combine_kernel(pos_hbm, ys_hbm, h_ref, p_ref, g_ref, o_ref, pos_s, buf, sem_p, sem, *, tm, pw, t0):
    i = pl.program_id(0)
    nt = pl.num_programs(0)
    slot = i % 2

    def fetch(tile, sl):
        cp = pltpu.make_async_copy(pos_hbm.at[t0 + tile], pos_s.at[sl], sem_p)
        cp.start()
        cp.wait()

        def issue(rb, carry):
            r0 = pl.multiple_of(rb * 8, 8)
            for s in range(8):
                for kk in range(TOP_K):
                    src_row = pos_s[sl, (r0 + s) * TOP_K + kk]
                    pltpu.make_async_copy(ys_hbm.at[pl.ds(src_row, 1)],
                                          buf.at[sl, pl.ds(kk * tm + r0 + s, 1)], sem.at[sl]).start(priority=kk % 2)
            return carry

        lax.fori_loop(0, tm // 8, issue, 0)

    @pl.when(i == 0)
    def _():
        fetch(0, 0)

    for sl in (0, 1):
        @pl.when(jnp.logical_and(i + 1 < nt, slot != sl))
        def _(sl=sl):
            fetch(i + 1, sl)

    pltpu.make_async_copy(ys_hbm.at[pl.ds(0, TOP_K * tm)], buf.at[slot], sem.at[slot]).wait()
    p = p_ref[...]
    parts = []
    for c in range(buf.shape[2] // pw):
        acc_hi = acc_lo = None
        for kk in range(TOP_K):
            hi, lo = _unpack_bf16_pair(buf[slot, pl.ds(kk * tm, tm), pl.ds(c * pw, pw)])
            w = p[:, kk:kk + 1]
            acc_hi = w * hi if acc_hi is None else acc_hi + w * hi
            acc_lo = w * lo if acc_lo is None else acc_lo + w * lo
        parts += [acc_hi, acc_lo]
    o_ref[...] = _rms(h_ref[...] + jnp.concatenate(parts, axis=-1), g_ref[...])


def combine(ys, pos, probs, h, g, pw, tm, row0, nrows):
    n, d = h.shape
    t0 = row0 // tm
    kern = functools.partial(_combine_kernel, tm=tm, pw=pw, t0=t0)
    return pl.pallas_call(
        kern,
        grid=(nrows // tm,),
        in_specs=[pl.BlockSpec(memory_space=pl.ANY), pl.BlockSpec(memory_space=pl.ANY),
                  pl.BlockSpec((tm, d), lambda i: (t0 + i, 0)),
                  pl.BlockSpec((tm, TOP_K), lambda i: (t0 + i, 0)), pl.BlockSpec((1, d), lambda i: (0, 0))],
        out_specs=pl.BlockSpec((tm, d), lambda i: (i, 0)),
        out_shape=jax.ShapeDtypeStruct((nrows, d), F32),
        scratch_shapes=[pltpu.SMEM((2, tm * TOP_K), jnp.int32), pltpu.VMEM((2, TOP_K * tm, d // 2), jnp.uint32),
                        pltpu.SemaphoreType.DMA, pltpu.SemaphoreType.DMA((2,))],
        compiler_params=_params("arbitrary"),
        name="moe_combine",
    )(pos.reshape(n // tm, tm * TOP_K), ys, h, probs, g.reshape(1, d))


def _group_tables(counts, top_i, rank, n_rows, tm):
    ne = counts.shape[0]
    padded = (counts + tm - 1) // tm * tm
    ends = jnp.cumsum(padded)
    offs = ends - padded
    pos = offs[top_i] + rank
    tile_start = jnp.arange(n_rows // tm, dtype=jnp.int32) * tm
    te = jnp.sum(tile_start[:, None] >= ends[None, :], axis=1).astype(jnp.int32)
    valid = tile_start < ends[-1]
    eidx = jnp.arange(ne, dtype=jnp.int32)
    nonempty = counts > 0
    first_e = jnp.min(jnp.where(nonempty, eidx, ne))
    last_e = jnp.max(jnp.where(nonempty, eidx, 0))
    later = jnp.logical_and(nonempty[None, :], eidx[None, :] > eidx[:, None])
    nxt_e = jnp.min(jnp.where(later, eidx[None, :], ne), axis=1)
    nxt_e = jnp.where(nxt_e == ne, first_e, nxt_e).astype(jnp.int32)
    te = jnp.where(valid, te, last_e).astype(jnp.int32)
    prev = jnp.concatenate([jnp.full((1,), -1, jnp.int32), te[:-1]])
    first = jnp.logical_and(valid, te != prev)
    tile_idx = jnp.arange(n_rows // tm, dtype=jnp.int32)
    meta = (te, valid.astype(jnp.int32), first.astype(jnp.int32), nxt_e[te],
            (te == last_e).astype(jnp.int32), tile_idx - (offs // tm)[te], (padded // tm)[te])
    return pos, meta


def moe_layer(h, xo, norm_g, router_w, router_b, w_gu, b_gu, w_dn, b_dn, final_g, n_first):
    n, d = h.shape
    ne = router_w.shape[1]
    h2, hn, top_i, probs, rank, counts = route(h, xo, norm_g, router_w, router_b)
    tm = 256 if n * TOP_K >= 256 * ne else 8
    n_rows = -(-(n * TOP_K) // tm) * tm + ne * tm
    pos, meta = _group_tables(counts[0], top_i, rank, n_rows, tm)
    xs = dispatch(hn, pos, n_rows)
    act = gmm(xs, w_gu, meta, tm, _tile(w_gu.shape[2], 2048), bias=b_gu, swiglu=True, packed=True, out_dtype=BF16)
    tn = _tile(d, 2048)
    ys = gmm(act, w_dn, meta, tm, tn, bias=b_dn, pack_out=True)
    pw = min(tn, 2 * V7X_MXU_COLS) // 2
    tm_c = math.gcd(_tile(n_first, 128), _tile(n - n_first, 128))
    return (combine(ys, pos, probs, h2, final_g, pw, tm_c, 0, n_first),
            combine(ys, pos, probs, h2, final_g, pw, tm_c, n_first, n - n_first))


def kernel(x_prompt, x_sample, mem_prompt, state_conv, state_lru, state_ret, cache_mem_k, cache_mem_v, norm_mix_g, w_in, conv_w, conv_b, lru_wa, lru_ba, lru_wx, lru_bx, lru_lambda, ret_norm_g, w_lru_branch, w_ret_branch, w_mix_out, norm_xa_g, norm_mem_g, xa_wq, xa_wk, xa_wv, xa_wo, norm_ffn_g, router_w, router_b, moe_w_gu, moe_b_gu, moe_w_dn, moe_b_dn, norm_final_g):
    depth = w_in.shape[0]
    assert depth == 1, "single-layer trunk"
    bp, tp, d = x_prompt.shape
    bs, ts, _ = x_sample.shape
    np_, ns = bp * tp, bs * ts
    n = np_ + ns
    n_mem = mem_prompt.shape[1]
    _, _, nh, dk, dv = state_ret.shape
    xh = cache_mem_k.shape[3]
    c_lru = conv_w.shape[-1]
    l = 0
    sizes = (c_lru, c_lru, nh * dk, nh * dk, nh * dv, nh * dv, d, d)
    cols = [0]
    for s in sizes:
        cols.append(cols[-1] + s)
    c_u, c_gl, c_q, c_k, c_v, c_gr, c_g1, c_g2 = cols[:8]
    assert c_u == 0 and c_gl == c_lru

    xp2, xs2 = x_prompt.reshape(np_, d), x_sample.reshape(ns, d)
    xn = rmsnorm2(xp2, xs2, norm_mix_g[l], BF16)
    proj = dense(xn, w_in)

    lru_args = (conv_w[l], conv_b[l], lru_wa[l], lru_ba[l], lru_wx[l], lru_bx[l], lru_lambda[l])
    y_lru = jnp.zeros((n, c_lru), BF16)
    y_lru, pconv, plru = lru_branch(proj, y_lru, 0, bp, tp, jnp.zeros((bp, CONV_W - 1, c_lru), F32),
                                    jnp.zeros((bp, c_lru), F32), *lru_args)
    y_lru, sconv, slru = lru_branch(proj, y_lru, np_, bs, ts, state_conv[l], state_lru[l], *lru_args)
    y_ret = jnp.zeros((n, nh * dv), BF16)
    y_ret, pret = retention_branch(proj, y_ret, 0, bp, tp, 0, jnp.zeros((bp, nh, dk, dv), F32), ret_norm_g[l],
                                   c_q, c_k, c_v, c_gr)
    y_ret, sret = retention_branch(proj, y_ret, np_, bs, ts, PAST_LEN, state_ret[l], ret_norm_g[l],
                                   c_q, c_k, c_v, c_gr)
    m1 = dense(y_lru, w_lru_branch)
    merged = dense(y_ret, w_ret_branch, BF16, gated_sum=(m1, proj, c_g1, c_g2))
    mix = dense(merged, w_mix_out)
    h, hn = add_rmsnorm2(xp2, xs2, mix, norm_xa_g[l])

    mn = rmsnorm(mem_prompt.reshape(bp * n_mem, d), norm_mem_g[l], BF16)
    mk_p = dense(mn, xa_wk)
    mv_p = dense(mn, xa_wv)
    q = dense(hn, xa_wq, BF16)
    o = jnp.zeros((n, d), BF16)
    o = cross_attention(q, o, 0, bp, tp, mk_p, mv_p, xh)
    o = cross_attention(q, o, np_, bs, ts, cache_mem_k[l].reshape(bs * n_mem, d),
                        cache_mem_v[l].reshape(bs * n_mem, d), xh)
    xo = dense(o, xa_wo)

    y_p, y_s = moe_layer(h, xo, norm_ffn_g[l], router_w[l], router_b[l], moe_w_gu[l], moe_b_gu[l], moe_w_dn[l],
                         moe_b_dn[l], norm_final_g, np_)

    hd = d // xh
    return (y_p.reshape(bp, tp, d), y_s.reshape(bs, ts, d),
            pconv[None], plru.reshape(1, bp, c_lru), pret[None],
            mk_p.reshape(1, bp, n_mem, xh, hd), mv_p.reshape(1, bp, n_mem, xh, hd),
            sconv[None], slru.reshape(1, bs, c_lru), sret[None])
```
